```python
import math
import jax, jax.numpy as jnp
from jax import lax
import numpy as np

D_MODEL = 1024
BATCH = 32
SEQ = 256
DEPTH = 4
DEC_BATCH = 8
DEC_SEQ = 2048
PAST_LEN = 256

GRID_W = 64
N_MIXERS = 3
N_A_LAYERS = (DEPTH + 2) // 3
N_B_LAYERS = (DEPTH + 1) // 3
N_C_LAYERS = DEPTH // 3

DEEPNORM_ALPHA = (2 * DEPTH) ** 0.25
DEEPNORM_BETA = (8 * DEPTH) ** -0.25
LN_EPS = 1e-5
RMS_EPS = 1e-6
FFN_RES = 0.5
N_MOD = 9

D_FF = 2816

A_D_INNER = 2 * D_MODEL
A_HEAD_DIM = 64
A_N_HEADS = A_D_INNER // A_HEAD_DIM
A_N_GROUPS = 4
A_D_STATE = 128
A_CONV_W = 5
A_CHUNK = 128
A_CONV_DIM = A_D_INNER + 2 * A_N_GROUPS * A_D_STATE
A_IN_DIM = A_D_INNER + A_CONV_DIM + 2 * A_N_HEADS

B_N_HEADS = 8
B_HEAD_K = 128
B_HEAD_V = 256
B_CONV_W = 5
B_CHUNK = 64
B_QK_DIM = B_N_HEADS * B_HEAD_K
B_V_DIM = B_N_HEADS * B_HEAD_V
B_CONV_DIM = 2 * B_QK_DIM + B_V_DIM
B_IN_DIM = B_CONV_DIM + B_V_DIM + 4 * B_N_HEADS

C_N_HEADS = 16
C_N_KV = 4
C_GROUP = C_N_HEADS // C_N_KV
C_HEAD_DIM = 64
C_WINDOW = 128
C_BLOCK = 128
C_IN_DIM = (C_N_HEADS + 2 * C_N_KV) * C_HEAD_DIM
C_SCALE = C_HEAD_DIM ** -0.5
ROPE_BASE = 10000.0

kernel_name = "hybrid_diffusion_prefix_trunk_step"


def layer_norm(x, g, b):
    xf = x.astype(jnp.float32)
    mu = jnp.mean(xf, axis=-1, keepdims=True)
    var = jnp.mean(jnp.square(xf - mu), axis=-1, keepdims=True)
    return ((xf - mu) * lax.rsqrt(var + LN_EPS) * g + b).astype(x.dtype)


def rms_norm(x, g):
    xf = x.astype(jnp.float32)
    return xf * lax.rsqrt(jnp.mean(jnp.square(xf), axis=-1, keepdims=True) + RMS_EPS) * g


def l2_normalize(x):
    xf = x.astype(jnp.float32)
    return xf * lax.rsqrt(jnp.sum(jnp.square(xf), axis=-1, keepdims=True) + RMS_EPS)


def adaln(cond, w, b):
    return (jax.nn.silu(cond) @ w + b).reshape(cond.shape[0], N_MOD, D_MODEL)


def modulate(x, shift, scale):
    return x * (1.0 + scale[:, None]) + shift[:, None]


def swiglu(h, w_gate, w_up, w_down):
    return (jax.nn.silu(h @ w_gate) * (h @ w_up)) @ w_down


def ffn_half(y, mod, s, g, b, w_gate, w_up, w_down):
    h = modulate(y, mod[:, 3 * s], mod[:, 3 * s + 1])
    f = swiglu(h, w_gate, w_up, w_down)
    return layer_norm(DEEPNORM_ALPHA * y + FFN_RES * mod[:, 3 * s + 2][:, None] * f, g, b)


def dwconv_centred(x, w, b):
    k = w.shape[0]
    y = lax.conv_general_dilated(x, w[:, None, :], window_strides=(1,), padding=[(k // 2, k // 2)],
                                 dimension_numbers=('NWC', 'WIO', 'NWC'), feature_group_count=x.shape[-1])
    return y + b


def axial_rope(x):
    f32 = jnp.float32
    L, d = x.shape[1], x.shape[-1]
    half = d // 2
    n_freq = half // 2
    t = jnp.arange(L)
    row = (t // GRID_W).astype(f32)
    col = (t % GRID_W).astype(f32)
    inv_freq = ROPE_BASE ** (-jnp.arange(n_freq, dtype=f32) / n_freq)

    def rot(xa, pos):
        ang = pos[:, None] * inv_freq
        cos = jnp.cos(ang)[None, :, None]
        sin = jnp.sin(ang)[None, :, None]
        x1, x2 = xa[..., :n_freq], xa[..., n_freq:]
        return jnp.concatenate([x1 * cos - x2 * sin, x1 * sin + x2 * cos], axis=-1)

    xf = x.astype(f32)
    return jnp.concatenate([rot(xf[..., :half], row), rot(xf[..., half:], col)], axis=-1).astype(x.dtype)


def ssd_chunked(x, dt, a, bm, cm, h0):
    f32 = jnp.float32
    n, L, H, P = x.shape
    G, N = bm.shape[-2:]
    R = H // G
    Q = A_CHUNK
    nc = L // Q
    x = x.astype(f32).reshape(n, nc, Q, G, R, P)
    dt = dt.astype(f32).reshape(n, nc, Q, G, R)
    bc = bm.astype(f32).reshape(n, nc, Q, G, N)
    cc = cm.astype(f32).reshape(n, nc, Q, G, N)
    acs = jnp.cumsum(dt * a.astype(f32).reshape(G, R), axis=2)
    xdt = x * dt[..., None]
    idx = jnp.arange(Q)
    causal = (idx[:, None] >= idx[None, :])[:, :, None, None]
    seg = acs[:, :, :, None] - acs[:, :, None, :]
    lmat = jnp.exp(jnp.where(causal, seg, -jnp.inf))
    cb = jnp.einsum('bcqgn,bcsgn->bcqsg', cc, bc)
    y_diag = jnp.einsum('bcqsg,bcqsgr,bcsgrp->bcqgrp', cb, lmat, xdt)
    decay_end = jnp.exp(acs[:, :, -1:] - acs)
    states = jnp.einsum('bcsgn,bcsgr,bcsgrp->bcgrpn', bc, decay_end, xdt)
    chunk_decay = jnp.exp(acs[:, :, -1])

    def step(h, inp):
        st, dec = inp
        return h * dec[..., None, None] + st, h

    h_fin, h_prev = lax.scan(step, h0.astype(f32).reshape(n, G, R, P, N),
                             (jnp.moveaxis(states, 1, 0), jnp.moveaxis(chunk_decay, 1, 0)))
    h_prev = jnp.moveaxis(h_prev, 0, 1)
    y_off = jnp.einsum('bcqgn,bcgrpn,bcqgr->bcqgrp', cc, h_prev, jnp.exp(acs))
    return (y_diag + y_off).reshape(n, L, H, P), h_fin.reshape(n, H, P, N)


def ssd_mixer(h, h0, w_in, conv_w, conv_b, dt_bias, a_log, d_skip, norm_g, w_out):
    f32 = jnp.float32
    n, L, _ = h.shape
    proj = h @ w_in
    z, xbc, dt = jnp.split(proj, [A_D_INNER, A_D_INNER + A_CONV_DIM], axis=-1)
    xbc = jax.nn.silu(dwconv_centred(xbc, conv_w, conv_b))
    xs, bm, cm = jnp.split(xbc, [A_D_INNER, A_D_INNER + A_N_GROUPS * A_D_STATE], axis=-1)
    xs = xs.reshape(n, L, A_N_HEADS, A_HEAD_DIM)
    bm = bm.reshape(n, L, A_N_GROUPS, A_D_STATE)
    cm = cm.reshape(n, L, A_N_GROUPS, A_D_STATE)
    dt = jax.nn.softplus(dt.reshape(n, L, 2, A_N_HEADS).astype(f32) + dt_bias)
    a = -jnp.exp(a_log.astype(f32))
    y_f, hf_f = ssd_chunked(xs, dt[:, :, 0], a[0], bm, cm, h0[:, 0])
    y_b, hf_b = ssd_chunked(xs[:, ::-1], dt[:, ::-1, 1], a[1], bm[:, ::-1], cm[:, ::-1], h0[:, 1])
    y = y_f + y_b[:, ::-1] + d_skip[:, None] * xs
    y = rms_norm(y.reshape(n, L, A_D_INNER) * jax.nn.silu(z.astype(f32)), norm_g)
    return y.astype(h.dtype) @ w_out, jnp.stack([hf_f, hf_b], axis=1)


def gdn_chunked(q, k, v, g, beta, s0):
    f32 = jnp.float32
    n, L, H, K = q.shape
    V = v.shape[-1]
    Q = B_CHUNK
    nc = L // Q

    def blocks(t):
        return jnp.moveaxis(t.astype(f32).reshape((n, nc, Q) + t.shape[2:]), 3, 2)

    q, k, v, g, beta = blocks(q), blocks(k), blocks(v), blocks(g), blocks(beta)
    gcs = jnp.cumsum(g, axis=-1)
    idx = jnp.arange(Q)
    incl = idx[:, None] >= idx[None, :]
    strict = idx[:, None] > idx[None, :]
    decay = jnp.exp(jnp.where(incl, gcs[..., :, None] - gcs[..., None, :], -jnp.inf))
    kk = jnp.einsum('bchik,bchjk->bchij', k, k)
    a_mat = jnp.where(strict, beta[..., :, None] * kk * decay, 0.0)
    rhs = jnp.concatenate([v * beta[..., None], k * (beta * jnp.exp(gcs))[..., None]], axis=-1)
    sol = lax.linalg.triangular_solve(a_mat + jnp.eye(Q, dtype=f32), rhs, left_side=True,
                                      lower=True, unit_diagonal=True)
    u, w = sol[..., :V], sol[..., V:]
    qk = jnp.einsum('bchik,bchjk->bchij', q, k) * decay
    q_dec = q * jnp.exp(gcs)[..., None]
    k_dec = k * jnp.exp(gcs[..., -1:] - gcs)[..., None]
    tot = jnp.exp(gcs[..., -1])

    def step(s, inp):
        u_c, w_c, q_c, k_c, t_c = inp
        delta = u_c - jnp.einsum('bhqk,bhkv->bhqv', w_c, s)
        o_inter = jnp.einsum('bhqk,bhkv->bhqv', q_c, s)
        s_new = s * t_c[..., None, None] + jnp.einsum('bhqk,bhqv->bhkv', k_c, delta)
        return s_new, (delta, o_inter)

    xs = (jnp.moveaxis(u, 1, 0), jnp.moveaxis(w, 1, 0), jnp.moveaxis(q_dec, 1, 0),
          jnp.moveaxis(k_dec, 1, 0), jnp.moveaxis(tot, 1, 0))
    s_fin, (delta, o_inter) = lax.scan(step, s0.astype(f32), xs)
    delta = jnp.moveaxis(delta, 0, 1)
    o = jnp.moveaxis(o_inter, 0, 1) + jnp.einsum('bchij,bchjv->bchiv', qk, delta)
    return jnp.moveaxis(o, 2, 3).reshape(n, L, H, V), s_fin


def gdn_mixer(h, s0, w_in, conv_w, conv_b, dt_bias, a_log, norm_g, w_out):
    f32 = jnp.float32
    n, L, _ = h.shape
    proj = h @ w_in
    qkv, z, ab = jnp.split(proj, [B_CONV_DIM, B_CONV_DIM + B_V_DIM], axis=-1)
    qkv = jax.nn.silu(dwconv_centred(qkv, conv_w, conv_b))
    q, k, v = jnp.split(qkv, [B_QK_DIM, 2 * B_QK_DIM], axis=-1)
    q = l2_normalize(q.reshape(n, L, B_N_HEADS, B_HEAD_K)) * (B_HEAD_K ** -0.5)
    k = l2_normalize(k.reshape(n, L, B_N_HEADS, B_HEAD_K))
    v = v.reshape(n, L, B_N_HEADS, B_HEAD_V)
    ab = ab.reshape(n, L, 2, 2, B_N_HEADS).astype(f32)
    beta = jax.nn.sigmoid(ab[:, :, 0])
    g = -jnp.exp(a_log.astype(f32)) * jax.nn.softplus(ab[:, :, 1] + dt_bias)
    o_f, s_f = gdn_chunked(q, k, v, g[:, :, 0], beta[:, :, 0], s0[:, 0])
    o_b, s_b = gdn_chunked(q[:, ::-1], k[:, ::-1], v[:, ::-1], g[:, ::-1, 1], beta[:, ::-1, 1], s0[:, 1])
    o = rms_norm(o_f + o_b[:, ::-1], norm_g) * jax.nn.silu(z.reshape(n, L, B_N_HEADS, B_HEAD_V).astype(f32))
    return o.reshape(n, L, B_V_DIM).astype(h.dtype) @ w_out, jnp.stack([s_f, s_b], axis=1)


def attn_qkv(h, w_in):
    n, L, _ = h.shape
    q, k, v = jnp.split(h @ w_in, [C_N_HEADS * C_HEAD_DIM, (C_N_HEADS + C_N_KV) * C_HEAD_DIM], axis=-1)
    return (q.reshape(n, L, C_N_HEADS, C_HEAD_DIM), k.reshape(n, L, C_N_KV, C_HEAD_DIM),
            v.reshape(n, L, C_N_KV, C_HEAD_DIM))


def sink_logits(sink, n, nq):
    s = sink.astype(jnp.float32).reshape(C_N_KV, C_GROUP)
    return jnp.broadcast_to(s[None, :, :, None, None], (n, C_N_KV, C_GROUP, nq, 1))


def attn_ctx(h, w_in, sink, w_out):
    n, L, _ = h.shape
    q, k, v = attn_qkv(h, w_in)
    q = q.reshape(n, L, C_N_KV, C_GROUP, C_HEAD_DIM)
    sinks = sink_logits(sink, n, C_BLOCK)

    def one_block(i):
        q_blk = lax.dynamic_slice_in_dim(q, i * C_BLOCK, C_BLOCK, axis=1)
        s = jnp.einsum('bqgrd,bkgd->bgrqk', q_blk, k).astype(jnp.float32) * C_SCALE
        p = jax.nn.softmax(jnp.concatenate([s, sinks], axis=-1), axis=-1)
        return jnp.einsum('bgrqk,bkgd->bqgrd', p[..., :L].astype(v.dtype), v)

    o = lax.map(one_block, jnp.arange(L // C_BLOCK))
    o = jnp.moveaxis(o, 0, 1).reshape(n, L, C_N_HEADS * C_HEAD_DIM)
    return o @ w_out, k, v


def attn_lat(h, k_ctx, v_ctx, w_in, sink, w_out):
    n, L, _ = h.shape
    lc = k_ctx.shape[1]
    q, k, v = attn_qkv(h, w_in)
    q = axial_rope(q).reshape(n, L, C_N_KV, C_GROUP, C_HEAD_DIM)
    k = axial_rope(k)
    pad = ((0, 0), (C_WINDOW, C_WINDOW), (0, 0), (0, 0))
    kp, vp = jnp.pad(k, pad), jnp.pad(v, pad)
    span = C_BLOCK + 2 * C_WINDOW
    sinks = sink_logits(sink, n, C_BLOCK)
    k_ctx = k_ctx.astype(k.dtype)
    v_ctx = v_ctx.astype(v.dtype)

    def one_block(i):
        start = i * C_BLOCK
        q_blk = lax.dynamic_slice_in_dim(q, start, C_BLOCK, axis=1)
        k_blk = lax.dynamic_slice_in_dim(kp, start, span, axis=1)
        v_blk = lax.dynamic_slice_in_dim(vp, start, span, axis=1)
        qpos = start + jnp.arange(C_BLOCK)
        kpos = start - C_WINDOW + jnp.arange(span)
        ok = (jnp.abs(qpos[:, None] - kpos[None, :]) <= C_WINDOW) & (kpos >= 0)[None, :] & (kpos < L)[None, :]
        s_loc = jnp.einsum('bqgrd,bkgd->bgrqk', q_blk, k_blk).astype(jnp.float32) * C_SCALE
        s_loc = jnp.where(ok, s_loc, -jnp.inf)
        s_ctx = jnp.einsum('bqgrd,bkgd->bgrqk', q_blk, k_ctx).astype(jnp.float32) * C_SCALE
        p = jax.nn.softmax(jnp.concatenate([s_loc, s_ctx, sinks], axis=-1), axis=-1)
        o = jnp.einsum('bgrqk,bkgd->bqgrd', p[..., :span].astype(v.dtype), v_blk)
        return o + jnp.einsum('bgrqk,bkgd->bqgrd', p[..., span:span + lc].astype(v.dtype), v_ctx)

    o = lax.map(one_block, jnp.arange(L // C_BLOCK))
    o = jnp.moveaxis(o, 0, 1).reshape(n, L, C_N_HEADS * C_HEAD_DIM)
    return o @ w_out


def setup_inputs(seed: int = 0) -> dict:
    key = jax.random.key(seed)
    ks = jax.random.split(key, 40)
    f32 = jnp.float32
    D = D_MODEL

    def nrm(k, shape, scale):
        return jax.random.normal(k, shape, f32) * scale

    def dt_bias_init(k, shape):
        dt = jnp.exp(jax.random.uniform(k, shape, f32) * (math.log(0.1) - math.log(0.001)) + math.log(0.001))
        return dt + jnp.log(-jnp.expm1(-dt))

    def a_log_init(k, shape):
        return jnp.log(jax.random.uniform(k, shape, f32, 1.0, 16.0))

    return {
        "x_prompt": nrm(ks[0], (BATCH, SEQ, D), 1.0),
        "x_sample": nrm(ks[1], (DEC_BATCH, DEC_SEQ, D), 1.0),
        "state_ssd": nrm(ks[2], (DEC_BATCH, N_A_LAYERS, 2, A_N_HEADS, A_HEAD_DIM, A_D_STATE), 0.1),
        "state_delta": nrm(ks[3], (DEC_BATCH, N_B_LAYERS, 2, B_N_HEADS, B_HEAD_K, B_HEAD_V), 0.3),
        "cache_k": nrm(ks[4], (DEC_BATCH, N_C_LAYERS, PAST_LEN, C_N_KV, C_HEAD_DIM), 1.0),
        "cache_v": nrm(ks[5], (DEC_BATCH, N_C_LAYERS, PAST_LEN, C_N_KV, C_HEAD_DIM), 1.0),
        "c": nrm(ks[6], (DEC_BATCH, D), 1.0),
        "c_ctx": nrm(ks[7], (D,), 1.0),
        "w_mod": nrm(ks[8], (DEPTH, D, N_MOD * D), 0.5 * D ** -0.5),
        "b_mod": nrm(ks[9], (DEPTH, N_MOD * D), 0.02),
        "ln_g": 1.0 + nrm(ks[10], (DEPTH, 3, D), 0.02),
        "ln_b": nrm(ks[11], (DEPTH, 3, D), 0.02),
        "ffn_w_gate": nrm(ks[12], (DEPTH, 2, D, D_FF), D ** -0.5),
        "ffn_w_up": nrm(ks[13], (DEPTH, 2, D, D_FF), D ** -0.5),
        "ffn_w_down": nrm(ks[14], (DEPTH, 2, D_FF, D), D_FF ** -0.5 * DEEPNORM_BETA),
        "ssd_w_in": nrm(ks[15], (N_A_LAYERS, D, A_IN_DIM), D ** -0.5),
        "ssd_conv_w": nrm(ks[16], (N_A_LAYERS, A_CONV_W, A_CONV_DIM), A_CONV_W ** -0.5),
        "ssd_conv_b": nrm(ks[17], (N_A_LAYERS, A_CONV_DIM), 0.02),
        "ssd_dt_bias": dt_bias_init(ks[18], (N_A_LAYERS, 2, A_N_HEADS)),
        "ssd_a_log": a_log_init(ks[19], (N_A_LAYERS, 2, A_N_HEADS)),
        "ssd_d": 1.0 + nrm(ks[20], (N_A_LAYERS, A_N_HEADS), 0.02),
        "ssd_norm": 1.0 + nrm(ks[21], (N_A_LAYERS, A_D_INNER), 0.02),
        "ssd_w_out": nrm(ks[22], (N_A_LAYERS, A_D_INNER, D), A_D_INNER ** -0.5 * DEEPNORM_BETA),
        "gdn_w_in": nrm(ks[23], (N_B_LAYERS, D, B_IN_DIM), D ** -0.5),
        "gdn_conv_w": nrm(ks[24], (N_B_LAYERS, B_CONV_W, B_CONV_DIM), B_CONV_W ** -0.5),
        "gdn_conv_b": nrm(ks[25], (N_B_LAYERS, B_CONV_DIM), 0.02),
        "gdn_dt_bias": dt_bias_init(ks[26], (N_B_LAYERS, 2, B_N_HEADS)),
        "gdn_a_log": a_log_init(ks[27], (N_B_LAYERS, 2, B_N_HEADS)),
        "gdn_norm": 1.0 + nrm(ks[28], (N_B_LAYERS, B_HEAD_V), 0.02),
        "gdn_w_out": nrm(ks[29], (N_B_LAYERS, B_V_DIM, D), B_V_DIM ** -0.5 * DEEPNORM_BETA),
        "attn_w_in": nrm(ks[30], (N_C_LAYERS, D, C_IN_DIM), D ** -0.5),
        "attn_sink": nrm(ks[31], (N_C_LAYERS, C_N_HEADS), 1.0),
        "attn_w_out": nrm(ks[32], (N_C_LAYERS, C_N_HEADS * C_HEAD_DIM, D), (C_N_HEADS * C_HEAD_DIM) ** -0.5 * DEEPNORM_BETA),
    }


def reference(x_prompt, x_sample, state_ssd, state_delta, cache_k, cache_v, c, c_ctx,
              w_mod, b_mod, ln_g, ln_b, ffn_w_gate, ffn_w_up, ffn_w_down,
              ssd_w_in, ssd_conv_w, ssd_conv_b, ssd_dt_bias, ssd_a_log, ssd_d, ssd_norm, ssd_w_out,
              gdn_w_in, gdn_conv_w, gdn_conv_b, gdn_dt_bias, gdn_a_log, gdn_norm, gdn_w_out,
              attn_w_in, attn_sink, attn_w_out):
    f32 = jnp.float32

    y = x_prompt
    n_p = x_prompt.shape[0]
    ssd_states, gdn_states, k_list, v_list = [], [], [], []
    for i in range(DEPTH):
        mod = adaln(c_ctx[None], w_mod[i], b_mod[i])
        y = ffn_half(y, mod, 0, ln_g[i, 0], ln_b[i, 0], ffn_w_gate[i, 0], ffn_w_up[i, 0], ffn_w_down[i, 0])
        h = modulate(y, mod[:, 3], mod[:, 4])
        kind, j = i % N_MIXERS, i // N_MIXERS
        if kind == 0:
            h0 = jnp.zeros((n_p, 2, A_N_HEADS, A_HEAD_DIM, A_D_STATE), f32)
            m, st = ssd_mixer(h, h0, ssd_w_in[j], ssd_conv_w[j], ssd_conv_b[j], ssd_dt_bias[j],
                              ssd_a_log[j], ssd_d[j], ssd_norm[j], ssd_w_out[j])
            ssd_states.append(st)
        elif kind == 1:
            s0 = jnp.zeros((n_p, 2, B_N_HEADS, B_HEAD_K, B_HEAD_V), f32)
            m, st = gdn_mixer(h, s0, gdn_w_in[j], gdn_conv_w[j], gdn_conv_b[j], gdn_dt_bias[j],
                              gdn_a_log[j], gdn_norm[j], gdn_w_out[j])
            gdn_states.append(st)
        else:
            m, kc, vc = attn_ctx(h, attn_w_in[j], attn_sink[j], attn_w_out[j])
            k_list.append(kc)
            v_list.append(vc)
        y = layer_norm(DEEPNORM_ALPHA * y + mod[:, 5][:, None] * m, ln_g[i, 1], ln_b[i, 1])
        y = ffn_half(y, mod, 2, ln_g[i, 2], ln_b[i, 2], ffn_w_gate[i, 1], ffn_w_up[i, 1], ffn_w_down[i, 1])
    y_prompt = y
    new_state_ssd = jnp.stack(ssd_states, axis=1)
    new_state_delta = jnp.stack(gdn_states, axis=1)
    new_cache_k = jnp.stack(k_list, axis=1)
    new_cache_v = jnp.stack(v_list, axis=1)

    y = x_sample
    for i in range(DEPTH):
        mod = adaln(c, w_mod[i], b_mod[i])
        y = ffn_half(y, mod, 0, ln_g[i, 0], ln_b[i, 0], ffn_w_gate[i, 0], ffn_w_up[i, 0], ffn_w_down[i, 0])
        h = modulate(y, mod[:, 3], mod[:, 4])
        kind, j = i % N_MIXERS, i // N_MIXERS
        if kind == 0:
            m, _ = ssd_mixer(h, state_ssd[:, j], ssd_w_in[j], ssd_conv_w[j], ssd_conv_b[j], ssd_dt_bias[j],
                             ssd_a_log[j], ssd_d[j], ssd_norm[j], ssd_w_out[j])
        elif kind == 1:
            m, _ = gdn_mixer(h, state_delta[:, j], gdn_w_in[j], gdn_conv_w[j], gdn_conv_b[j], gdn_dt_bias[j],
                             gdn_a_log[j], gdn_norm[j], gdn_w_out[j])
        else:
            m = attn_lat(h, cache_k[:, j], cache_v[:, j], attn_w_in[j], attn_sink[j], attn_w_out[j])
        y = layer_norm(DEEPNORM_ALPHA * y + mod[:, 5][:, None] * m, ln_g[i, 1], ln_b[i, 1])
        y = ffn_half(y, mod, 2, ln_g[i, 2], ln_b[i, 2], ffn_w_gate[i, 1], ffn_w_up[i, 1], ffn_w_down[i, 1])
    y_sample = y

    return (y_prompt, y_sample, new_state_ssd, new_state_delta, new_cache_k, new_cache_v)
```

```python
import functools
import math

import jax
import jax.numpy as jnp
from jax import lax
from jax.experimental import pallas as pl
from jax.experimental.pallas import tpu as pltpu

F32 = jnp.float32
BF16 = jnp.bfloat16

D_MODEL = 1024
DEPTH = 4
GRID_W = 64
N_MOD = 9
D_FF = 2816
DEEPNORM_ALPHA = (2 * DEPTH) ** 0.25
LN_EPS = 1e-5
RMS_EPS = 1e-6
FFN_RES = 0.5

A_D_INNER = 2 * D_MODEL
A_HEAD_DIM = 64
A_N_HEADS = A_D_INNER // A_HEAD_DIM
A_N_GROUPS = 4
A_HEADS_PER_GROUP = A_N_HEADS // A_N_GROUPS
A_GROUP_DIM = A_HEADS_PER_GROUP * A_HEAD_DIM
A_D_STATE = 128
A_CHUNK = 128
A_CONV_DIM = A_D_INNER + 2 * A_N_GROUPS * A_D_STATE

B_N_HEADS = 8
B_HEAD_K = 128
B_HEAD_V = 256
B_CHUNK = 64
B_QK_DIM = B_N_HEADS * B_HEAD_K
B_V_DIM = B_N_HEADS * B_HEAD_V
B_CONV_DIM = 2 * B_QK_DIM + B_V_DIM

C_N_HEADS = 16
C_N_KV = 4
C_GROUP = C_N_HEADS // C_N_KV
C_HEAD_DIM = 64
C_WINDOW = 128
C_BLOCK = 128
C_Q_DIM = C_N_HEADS * C_HEAD_DIM
C_KV_DIM = C_N_KV * C_HEAD_DIM
C_SCALE = C_HEAD_DIM ** -0.5
ROPE_BASE = 10000.0

N_COND = 16
ROW_TILE = 512
PROJ_TILE = 512
FF_TILE = 256
CONV_ROWS = 2048
CONV_COLS = 256
VMEM_LIMIT = 56 * 1024 * 1024


def _cparams(n_axes):
    return pltpu.CompilerParams(dimension_semantics=("arbitrary",) * n_axes, vmem_limit_bytes=VMEM_LIMIT)


def _dot(a, b, precision=None):
    return lax.dot_general(a, b, (((1,), (0,)), ((), ())), precision=precision, preferred_element_type=F32)


def _dot_nt(a, b, precision=None):
    return lax.dot_general(a, b, (((1,), (1,)), ((), ())), precision=precision, preferred_element_type=F32)


def _dot_tn(a, b, precision=None):
    return lax.dot_general(a, b, (((0,), (0,)), ((), ())), precision=precision, preferred_element_type=F32)


def _sigmoid(x):
    return 1.0 / (1.0 + jnp.exp(-x))


def _silu(x):
    return x * _sigmoid(x)


def _softplus(x):
    return jnp.maximum(x, 0.0) + jnp.log(1.0 + jnp.exp(-jnp.abs(x)))


def _layer_norm(t, g, b):
    mu = jnp.mean(t, axis=-1, keepdims=True)
    tc = t - mu
    var = jnp.mean(tc * tc, axis=-1, keepdims=True)
    return tc * lax.rsqrt(var + LN_EPS) * g + b


def _cond_index(i, n_prompt_tiles, tiles_per_request):
    return jnp.where(i < n_prompt_tiles, 0, 1 + (jnp.maximum(i - n_prompt_tiles, 0)) // tiles_per_request)


def _adaln_kernel(c_ref, w_ref, b_ref, o_ref):
    c = c_ref[...]
    h = _silu(c).astype(BF16)
    o_ref[...] = _dot(h, w_ref[...].astype(BF16)) + b_ref[...]


def _adaln(cond, w_mod, b_mod):
    n_out = N_MOD * D_MODEL
    tn = D_MODEL
    return pl.pallas_call(
        _adaln_kernel,
        grid=(DEPTH, n_out // tn),
        in_specs=[
            pl.BlockSpec((N_COND, D_MODEL), lambda l, j: (0, 0)),
            pl.BlockSpec((None, D_MODEL, tn), lambda l, j: (l, 0, j)),
            pl.BlockSpec((None, 1, tn), lambda l, j: (l, 0, j)),
        ],
        out_specs=pl.BlockSpec((None, N_COND, tn), lambda l, j: (l, 0, j)),
        out_shape=jax.ShapeDtypeStruct((DEPTH, N_COND, n_out), F32),
        compiler_params=_cparams(2),
        name="adaln",
    )(cond, w_mod, b_mod.reshape(DEPTH, 1, n_out))


def _ffn_kernel(y_ref, mod_ref, g_ref, b_ref, wg_ref, wu_ref, wd_ref, o_ref, h_scr, acc_scr, *, s, n_ff):
    j = pl.program_id(1)

    @pl.when(j == 0)
    def _():
        shift = mod_ref[pl.ds(3 * s, 1), :]
        scale = mod_ref[pl.ds(3 * s + 1, 1), :]
        h_scr[...] = (y_ref[...] * (1.0 + scale) + shift).astype(BF16)
        acc_scr[...] = jnp.zeros_like(acc_scr)

    h = h_scr[...]
    a = _dot(h, wg_ref[...])
    u = _dot(h, wu_ref[...])
    f = (_silu(a) * u).astype(BF16)
    acc_scr[...] += _dot(f, wd_ref[...])

    @pl.when(j == n_ff - 1)
    def _():
        gate = mod_ref[pl.ds(3 * s + 2, 1), :]
        t = DEEPNORM_ALPHA * y_ref[...] + (FFN_RES * gate) * acc_scr[...]
        o_ref[...] = _layer_norm(t, g_ref[...], b_ref[...])


def _ffn_half(y, mod, s, g, b, w_gate, w_up, w_down, cond_of_tile):
    m = y.shape[0]
    n_ff = D_FF // FF_TILE
    return pl.pallas_call(
        functools.partial(_ffn_kernel, s=s, n_ff=n_ff),
        grid=(m // ROW_TILE, n_ff),
        in_specs=[
            pl.BlockSpec((ROW_TILE, D_MODEL), lambda i, j: (i, 0)),
            pl.BlockSpec((None, N_MOD, D_MODEL), lambda i, j: (cond_of_tile(i), 0, 0)),
            pl.BlockSpec((1, D_MODEL), lambda i, j: (0, 0)),
            pl.BlockSpec((1, D_MODEL), lambda i, j: (0, 0)),
            pl.BlockSpec((D_MODEL, FF_TILE), lambda i, j: (0, j)),
            pl.BlockSpec((D_MODEL, FF_TILE), lambda i, j: (0, j)),
            pl.BlockSpec((FF_TILE, D_MODEL), lambda i, j: (j, 0)),
        ],
        out_specs=pl.BlockSpec((ROW_TILE, D_MODEL), lambda i, j: (i, 0)),
        out_shape=jax.ShapeDtypeStruct((m, D_MODEL), F32),
        scratch_shapes=[pltpu.VMEM((ROW_TILE, D_MODEL), BF16), pltpu.VMEM((ROW_TILE, D_MODEL), F32)],
        compiler_params=_cparams(2),
        name="ffn_half",
    )(y, mod, g.reshape(1, -1), b.reshape(1, -1), w_gate.astype(BF16), w_up.astype(BF16), w_down.astype(BF16))


def _modlin_kernel(y_ref, mod_ref, w_ref, o_ref):
    shift = mod_ref[pl.ds(3, 1), :]
    scale = mod_ref[pl.ds(4, 1), :]
    h = (y_ref[...] * (1.0 + scale) + shift).astype(BF16)
    o_ref[...] = _dot(h, w_ref[...])


def _mod_linear(y, mod, w, cond_of_tile):
    m = y.shape[0]
    n = w.shape[1]
    n_pad = -(-n // PROJ_TILE) * PROJ_TILE
    w = jnp.pad(w.astype(BF16), ((0, 0), (0, n_pad - n)))
    return pl.pallas_call(
        _modlin_kernel,
        grid=(m // ROW_TILE, n_pad // PROJ_TILE),
        in_specs=[
            pl.BlockSpec((ROW_TILE, D_MODEL), lambda i, j: (i, 0)),
            pl.BlockSpec((None, N_MOD, D_MODEL), lambda i, j: (cond_of_tile(i), 0, 0)),
            pl.BlockSpec((D_MODEL, PROJ_TILE), lambda i, j: (0, j)),
        ],
        out_specs=pl.BlockSpec((ROW_TILE, PROJ_TILE), lambda i, j: (i, j)),
        out_shape=jax.ShapeDtypeStruct((m, n_pad), F32),
        compiler_params=_cparams(2),
        name="mod_linear",
    )(y, mod, w)


def _outproj_kernel(a_ref, y_ref, mod_ref, w_ref, g_ref, b_ref, *rest, rms):
    if rms:
        ng_ref, o_ref = rest
    else:
        (o_ref,) = rest
    a = a_ref[...]
    if rms:
        a = a * lax.rsqrt(jnp.mean(a * a, axis=-1, keepdims=True) + RMS_EPS) * ng_ref[...]
    mix = _dot(a.astype(BF16), w_ref[...])
    gate = mod_ref[pl.ds(5, 1), :]
    t = DEEPNORM_ALPHA * y_ref[...] + gate * mix
    o_ref[...] = _layer_norm(t, g_ref[...], b_ref[...])


def _out_proj_ln(a, y, mod, w, g, b, cond_of_tile, rms_g=None):
    m, k = a.shape
    in_specs = [
        pl.BlockSpec((ROW_TILE, k), lambda i: (i, 0)),
        pl.BlockSpec((ROW_TILE, D_MODEL), lambda i: (i, 0)),
        pl.BlockSpec((None, N_MOD, D_MODEL), lambda i: (cond_of_tile(i), 0, 0)),
        pl.BlockSpec((k, D_MODEL), lambda i: (0, 0)),
        pl.BlockSpec((1, D_MODEL), lambda i: (0, 0)),
        pl.BlockSpec((1, D_MODEL), lambda i: (0, 0)),
    ]
    args = [a, y, mod, w.astype(BF16), g.reshape(1, -1), b.reshape(1, -1)]
    if rms_g is not None:
        in_specs.append(pl.BlockSpec((1, k), lambda i: (0, 0)))
        args.append(rms_g.reshape(1, -1))
    return pl.pallas_call(
        functools.partial(_outproj_kernel, rms=rms_g is not None),
        grid=(m // ROW_TILE,),
        in_specs=in_specs,
        out_specs=pl.BlockSpec((ROW_TILE, D_MODEL), lambda i: (i, 0)),
        out_shape=jax.ShapeDtypeStruct((m, D_MODEL), F32),
        compiler_params=_cparams(1),
        name="out_proj_ln",
    )(*args)


def _conv_kernel(x_ref, w_ref, b_ref, o_ref, *, n_prompt_tiles, len_prompt, len_latent):
    is_prompt = pl.program_id(0) < n_prompt_tiles
    seq_len = jnp.where(is_prompt, len_prompt, len_latent)
    x = x_ref[...]
    rows = x.shape[0]
    row = lax.broadcasted_iota(jnp.int32, (rows, 1), 0)
    pos = jnp.where(is_prompt, row % len_prompt, row % len_latent)
    acc = b_ref[...] + w_ref[pl.ds(2, 1), :] * x
    for k in (0, 1, 3, 4):
        d = k - 2
        shifted = pltpu.roll(x, (rows - d) % rows, 0)
        valid = jnp.logical_and(pos + d >= 0, pos + d < seq_len)
        acc = acc + w_ref[pl.ds(k, 1), :] * jnp.where(valid, shifted, 0.0)
    o_ref[...] = _silu(acc)


def _conv_silu(proj, col0, width, conv_w, conv_b, n_prompt_rows, len_prompt, len_latent):
    m = proj.shape[0]
    c0 = col0 // CONV_COLS
    return pl.pallas_call(
        functools.partial(_conv_kernel, n_prompt_tiles=n_prompt_rows // CONV_ROWS,
                          len_prompt=len_prompt, len_latent=len_latent),
        grid=(m // CONV_ROWS, width // CONV_COLS),
        in_specs=[
            pl.BlockSpec((CONV_ROWS, CONV_COLS), lambda i, j: (i, c0 + j)),
            pl.BlockSpec((5, CONV_COLS), lambda i, j: (0, j)),
            pl.BlockSpec((1, CONV_COLS), lambda i, j: (0, j)),
        ],
        out_specs=pl.BlockSpec((CONV_ROWS, CONV_COLS), lambda i, j: (i, j)),
        out_shape=jax.ShapeDtypeStruct((m, width), F32),
        compiler_params=_cparams(2),
        name="conv_silu",
    )(proj, conv_w, conv_b.reshape(1, -1))


def _ssd_kernel(*refs, seq_len, has_h0, emit_state):
    x_ref, z_ref, b_ref, c_ref, dt_ref, dtt_ref, bias_ref, biast_ref, alog_ref, alogt_ref, dskip_ref = refs[:11]
    pos = 11
    h0_ref = None
    if has_h0:
        h0_ref = refs[pos]
        pos += 2
    y_ref = refs[pos]
    pos += 1
    hfin_ref = None
    if emit_state:
        hfin_ref = refs[pos]
        pos += 1
    s_scr = refs[pos]

    q = A_CHUNK
    nc = seq_len // q
    hpg = A_HEADS_PER_GROUP
    hd = A_HEAD_DIM
    ii = lax.broadcasted_iota(jnp.int32, (q, q), 0)
    jj = lax.broadcasted_iota(jnp.int32, (q, q), 1)
    lower = ii >= jj
    upper = ii <= jj
    t_lower = jnp.where(lower, 1.0, 0.0).astype(F32)
    t_upper = jnp.where(upper, 1.0, 0.0).astype(F32)
    er = lax.broadcasted_iota(jnp.int32, (hpg, hpg * hd), 0)
    ec = lax.broadcasted_iota(jnp.int32, (hpg, hpg * hd), 1)
    expand = jnp.where(ec // hd == er, 1.0, 0.0).astype(F32)
    hi = lax.Precision.HIGHEST

    y_ref[...] = dskip_ref[...] * x_ref[...]

    for d in range(2):
        bias = bias_ref[:, d * hpg:(d + 1) * hpg]
        a_neg = -jnp.exp(alog_ref[:, d * hpg:(d + 1) * hpg])
        bias_t = biast_ref[d * hpg:(d + 1) * hpg, :]
        a_neg_t = -jnp.exp(alogt_ref[d * hpg:(d + 1) * hpg, :])
        if has_h0:
            s_scr[...] = h0_ref[d]
        else:
            s_scr[...] = jnp.zeros_like(s_scr)
        mask = lower if d == 0 else upper

        def body(ci, carry, d=d, bias=bias, a_neg=a_neg, bias_t=bias_t, a_neg_t=a_neg_t, mask=mask):
            c = ci if d == 0 else nc - 1 - ci
            r0 = pl.multiple_of(c * q, q)
            dtc = _softplus(dt_ref[pl.ds(r0, q), d * hpg:(d + 1) * hpg] + bias)
            dta = dtc * a_neg
            dtc_t = _softplus(dtt_ref[c][d * hpg:(d + 1) * hpg, :] + bias_t)
            dta_t = dtc_t * a_neg_t
            if d == 0:
                acs = _dot(t_lower, dta, hi)
                acs_t = _dot(dta_t, t_upper, hi)
            else:
                acs = _dot(t_upper, dta, hi)
                acs_t = _dot(dta_t, t_lower, hi)
            xc = x_ref[pl.ds(r0, q), :]
            bb = b_ref[pl.ds(r0, q), :].astype(BF16)
            cc = c_ref[pl.ds(r0, q), :].astype(BF16)
            cb = _dot_nt(cc, bb)
            dt_w = _dot(dtc, expand, hi)
            acs_w = _dot(acs, expand, hi)
            tot_w = acs_w[q - 1:q, :] if d == 0 else acs_w[0:1, :]
            xdt = xc * dt_w
            xdt_b = xdt.astype(BF16)
            y_off = jnp.exp(acs_w) * _dot(cc, s_scr[...].astype(BF16))
            y_diag = []
            for r in range(hpg):
                seg = acs[:, r:r + 1] - acs_t[r:r + 1, :]
                lmat = jnp.exp(jnp.where(mask, seg, -jnp.inf))
                mm = (cb * lmat).astype(BF16)
                y_diag.append(_dot(mm, xdt_b[:, r * hd:(r + 1) * hd]))
            y_ref[pl.ds(r0, q), :] += jnp.concatenate(y_diag, axis=1) + y_off
            xs = (xdt * jnp.exp(tot_w - acs_w)).astype(BF16)
            s_scr[...] = s_scr[...] * jnp.exp(tot_w) + _dot_tn(bb, xs)
            return carry

        lax.fori_loop(0, nc, body, 0)
        if emit_state:
            hfin_ref[d] = s_scr[...]

    y_ref[...] = y_ref[...] * _silu(z_ref[...])


def _ssd_core(proj, xbc, dt_col, dt_row, params, h0, prev, row0, n_seq, seq_len, emit_state):
    m = proj.shape[0]
    gd = A_GROUP_DIM
    ns = A_D_STATE
    hpg = A_HEADS_PER_GROUP
    sb = row0 // seq_len
    cb0 = row0 // A_CHUNK
    nc = seq_len // A_CHUNK
    x_blocks = A_D_INNER // gd
    bias_c, bias_r, alog_c, alog_r, dskip = params
    in_specs = [
        pl.BlockSpec((seq_len, gd), lambda n, g: (sb + n, g)),
        pl.BlockSpec((seq_len, gd), lambda n, g: (sb + n, g)),
        pl.BlockSpec((seq_len, ns), lambda n, g: (sb + n, x_blocks * (gd // ns) + g)),
        pl.BlockSpec((seq_len, ns), lambda n, g: (sb + n, x_blocks * (gd // ns) + A_N_GROUPS + g)),
        pl.BlockSpec((None, seq_len, 2 * hpg), lambda n, g: (g, sb + n, 0)),
        pl.BlockSpec((None, nc, 2 * hpg, A_CHUNK), lambda n, g: (g, sb + n, 0, 0)),
        pl.BlockSpec((None, 1, 2 * hpg), lambda n, g: (g, 0, 0)),
        pl.BlockSpec((None, 2 * hpg, 1), lambda n, g: (g, 0, 0)),
        pl.BlockSpec((None, 1, 2 * hpg), lambda n, g: (g, 0, 0)),
        pl.BlockSpec((None, 2 * hpg, 1), lambda n, g: (g, 0, 0)),
        pl.BlockSpec((None, 1, gd), lambda n, g: (g, 0, 0)),
    ]
    args = [xbc, proj, xbc, xbc, dt_col, dt_row, bias_c, bias_r, alog_c, alog_r, dskip]
    aliases = {}
    if h0 is not None:
        in_specs += [pl.BlockSpec((None, 2, None, ns, gd), lambda n, g: (n, 0, g, 0, 0)),
                     pl.BlockSpec(memory_space=pl.ANY)]
        args += [h0, prev]
        aliases = {len(args) - 1: 0}
    out_specs = [pl.BlockSpec((seq_len, gd), lambda n, g: (sb + n, g))]
    out_shape = [jax.ShapeDtypeStruct((m, A_D_INNER), F32)]
    if emit_state:
        out_specs.append(pl.BlockSpec((None, 2, None, ns, gd), lambda n, g: (n, 0, g, 0, 0)))
        out_shape.append(jax.ShapeDtypeStruct((n_seq, 2, A_N_GROUPS, ns, gd), F32))
    res = pl.pallas_call(
        functools.partial(_ssd_kernel, seq_len=seq_len, has_h0=h0 is not None, emit_state=emit_state),
        grid=(n_seq, A_N_GROUPS),
        in_specs=in_specs,
        out_specs=out_specs,
        out_shape=out_shape,
        scratch_shapes=[pltpu.VMEM((ns, gd), F32)],
        input_output_aliases=aliases,
        compiler_params=_cparams(2),
        name="ssd_core",
    )(*args)
    return res if emit_state else (res[0], None)


def _ssd_mixer(y, mod, cond_of_tile, w_in, conv_w, conv_b, dt_bias, a_log, d_skip, state0, n_prompt, len_prompt,
               n_latent, len_latent):
    m = y.shape[0]
    mp = n_prompt * len_prompt
    proj = _mod_linear(y, mod, w_in, cond_of_tile)
    xbc = _conv_silu(proj, A_D_INNER, A_CONV_DIM, conv_w, conv_b, mp, len_prompt, len_latent)
    hpg = A_HEADS_PER_GROUP
    dt0 = A_D_INNER + A_CONV_DIM
    dt_raw = proj[:, dt0:dt0 + 2 * A_N_HEADS]
    dt_col = dt_raw.reshape(m, 2, A_N_GROUPS, hpg).transpose(2, 0, 1, 3).reshape(A_N_GROUPS, m, 2 * hpg)
    dt_row = dt_raw.reshape(m // A_CHUNK, A_CHUNK, 2, A_N_GROUPS, hpg).transpose(3, 0, 2, 4, 1)
    dt_row = dt_row.reshape(A_N_GROUPS, m // A_CHUNK, 2 * hpg, A_CHUNK)

    def per_group(p):
        return p.reshape(2, A_N_GROUPS, hpg).transpose(1, 0, 2).reshape(A_N_GROUPS, 2 * hpg)

    bias_g = per_group(dt_bias.astype(F32))
    alog_g = per_group(a_log.astype(F32))
    dskip = jnp.repeat(d_skip.astype(F32), A_HEAD_DIM).reshape(A_N_GROUPS, 1, A_GROUP_DIM)
    params = (bias_g[:, None, :], bias_g[:, :, None], alog_g[:, None, :], alog_g[:, :, None], dskip)
    h0 = state0.reshape(n_latent, 2, A_N_GROUPS, hpg, A_HEAD_DIM, A_D_STATE).transpose(0, 1, 2, 5, 3, 4)
    h0 = h0.reshape(n_latent, 2, A_N_GROUPS, A_D_STATE, A_GROUP_DIM)
    out, st = _ssd_core(proj, xbc, dt_col, dt_row, params, None, None, 0, n_prompt, len_prompt, True)
    out, _ = _ssd_core(proj, xbc, dt_col, dt_row, params, h0, out, mp, n_latent, len_latent, False)
    st = st.reshape(n_prompt, 2, A_N_GROUPS, A_D_STATE, hpg, A_HEAD_DIM).transpose(0, 1, 2, 4, 5, 3)
    return out, st.reshape(n_prompt, 2, A_N_HEADS, A_HEAD_DIM, A_D_STATE)


def _gdn_kernel(*refs, seq_len, has_s0, emit_state):
    q_ref, k_ref, v_ref, z_ref, ab_ref, abt_ref, dtb_ref, alog_ref, ng_ref = refs[:9]
    pos = 9
    s0_ref = None
    if has_s0:
        s0_ref = refs[pos]
        pos += 2
    o_ref = refs[pos]
    pos += 1
    sfin_ref = None
    if emit_state:
        sfin_ref = refs[pos]
        pos += 1
    qn_scr, kn_scr, s_scr = refs[pos:pos + 3]

    h = pl.program_id(1)
    ch = B_CHUNK
    nc = seq_len // ch
    hi = lax.Precision.HIGHEST
    ii = lax.broadcasted_iota(jnp.int32, (ch, ch), 0)
    jj = lax.broadcasted_iota(jnp.int32, (ch, ch), 1)
    eye = jnp.where(ii == jj, 1.0, 0.0).astype(F32)

    qv = q_ref[...]
    qn_scr[...] = qv * lax.rsqrt(jnp.sum(qv * qv, axis=-1, keepdims=True) + RMS_EPS) * (B_HEAD_K ** -0.5)
    kv = k_ref[...]
    kn_scr[...] = kv * lax.rsqrt(jnp.sum(kv * kv, axis=-1, keepdims=True) + RMS_EPS)
    o_ref[...] = jnp.zeros_like(o_ref)

    for d in range(2):
        dt_bias = dtb_ref[d, h]
        a_neg = -jnp.exp(alog_ref[d, h])
        if has_s0:
            s_scr[...] = s0_ref[d]
        else:
            s_scr[...] = jnp.zeros_like(s_scr)
        incl = (ii >= jj) if d == 0 else (ii <= jj)
        incl_t = (ii <= jj) if d == 0 else (ii >= jj)
        strict = (ii > jj) if d == 0 else (ii < jj)

        def body(ci, carry, d=d, dt_bias=dt_bias, a_neg=a_neg, incl=incl, incl_t=incl_t, strict=strict):
            c = ci if d == 0 else nc - 1 - ci
            r0 = pl.multiple_of(c * ch, ch)
            ab = ab_ref[pl.ds(r0, ch), :]
            abt = abt_ref[c]
            beta = _sigmoid(ab[:, d:d + 1])
            g_col = a_neg * _softplus(ab[:, 2 + d:3 + d] + dt_bias)
            g_row = a_neg * _softplus(abt[2 + d:3 + d, :] + dt_bias)
            gcs = jnp.sum(jnp.where(incl, jnp.broadcast_to(g_row, (ch, ch)), 0.0), axis=1, keepdims=True)
            gcs_t = jnp.sum(jnp.where(incl_t, jnp.broadcast_to(g_col, (ch, ch)), 0.0), axis=0, keepdims=True)
            tot = gcs[ch - 1:ch, :] if d == 0 else gcs[0:1, :]
            decay = jnp.exp(jnp.where(incl, gcs - gcs_t, -jnp.inf))
            kc = kn_scr[pl.ds(r0, ch), :]
            qc = qn_scr[pl.ds(r0, ch), :]
            vc = v_ref[pl.ds(r0, ch), :]
            kcb = kc.astype(BF16)
            kk = _dot_nt(kcb, kcb)
            a_mat = jnp.where(strict, beta * kk * decay, 0.0)
            inv = eye - a_mat
            pw = a_mat
            for _ in range(5):
                pw = _dot(pw, pw, hi)
                inv = inv + _dot(inv, pw, hi)
            inv_b = inv.astype(BF16)
            e_gcs = jnp.exp(gcs)
            u = _dot(inv_b, (vc * beta).astype(BF16))
            w = _dot(inv_b, (kc * (beta * e_gcs)).astype(BF16))
            s_b = s_scr[...].astype(BF16)
            delta = u - _dot(w.astype(BF16), s_b)
            delta_b = delta.astype(BF16)
            o_inter = _dot((qc * e_gcs).astype(BF16), s_b)
            qk = _dot_nt(qc.astype(BF16), kcb) * decay
            o_ref[pl.ds(r0, ch), :] += o_inter + _dot(qk.astype(BF16), delta_b)
            k_dec = (kc * jnp.exp(tot - gcs)).astype(BF16)
            s_scr[...] = s_scr[...] * jnp.exp(tot) + _dot_tn(k_dec, delta_b)
            return carry

        lax.fori_loop(0, nc, body, 0)
        if emit_state:
            sfin_ref[d] = s_scr[...]

    o = o_ref[...]
    o = o * lax.rsqrt(jnp.mean(o * o, axis=-1, keepdims=True) + RMS_EPS) * ng_ref[...]
    o_ref[...] = o * _silu(z_ref[...])


def _gdn_core(proj, qkv, ab_col, ab_row, dt_bias, a_log, norm_g, s0, prev, row0, n_seq, seq_len, emit_state):
    m = proj.shape[0]
    hk, hv = B_HEAD_K, B_HEAD_V
    sb = row0 // seq_len
    nc = seq_len // B_CHUNK
    z_block0 = B_CONV_DIM // hv
    in_specs = [
        pl.BlockSpec((seq_len, hk), lambda n, h: (sb + n, h)),
        pl.BlockSpec((seq_len, hk), lambda n, h: (sb + n, B_N_HEADS + h)),
        pl.BlockSpec((seq_len, hv), lambda n, h: (sb + n, (2 * B_QK_DIM) // hv + h)),
        pl.BlockSpec((seq_len, hv), lambda n, h: (sb + n, z_block0 + h)),
        pl.BlockSpec((None, seq_len, 4), lambda n, h: (h, sb + n, 0)),
        pl.BlockSpec((None, nc, 4, B_CHUNK), lambda n, h: (h, sb + n, 0, 0)),
        pl.BlockSpec(memory_space=pltpu.SMEM),
        pl.BlockSpec(memory_space=pltpu.SMEM),
        pl.BlockSpec((1, hv), lambda n, h: (0, 0)),
    ]
    args = [qkv, qkv, qkv, proj, ab_col, ab_row, dt_bias, a_log, norm_g.reshape(1, hv)]
    aliases = {}
    if s0 is not None:
        in_specs += [pl.BlockSpec((None, 2, None, hk, hv), lambda n, h: (n, 0, h, 0, 0)),
                     pl.BlockSpec(memory_space=pl.ANY)]
        args += [s0, prev]
        aliases = {len(args) - 1: 0}
    out_specs = [pl.BlockSpec((seq_len, hv), lambda n, h: (sb + n, h))]
    out_shape = [jax.ShapeDtypeStruct((m, B_V_DIM), F32)]
    if emit_state:
        out_specs.append(pl.BlockSpec((None, 2, None, hk, hv), lambda n, h: (n, 0, h, 0, 0)))
        out_shape.append(jax.ShapeDtypeStruct((n_seq, 2, B_N_HEADS, hk, hv), F32))
    res = pl.pallas_call(
        functools.partial(_gdn_kernel, seq_len=seq_len, has_s0=s0 is not None, emit_state=emit_state),
        grid=(n_seq, B_N_HEADS),
        in_specs=in_specs,
        out_specs=out_specs,
        out_shape=out_shape,
        scratch_shapes=[pltpu.VMEM((seq_len, hk), F32), pltpu.VMEM((seq_len, hk), F32), pltpu.VMEM((hk, hv), F32)],
        input_output_aliases=aliases,
        compiler_params=_cparams(2),
        name="gdn_core",
    )(*args)
    return res if emit_state else (res[0], None)


def _gdn_mixer(y, mod, cond_of_tile, w_in, conv_w, conv_b, dt_bias, a_log, norm_g, state0, n_prompt, len_prompt,
               n_latent, len_latent):
    m = y.shape[0]
    mp = n_prompt * len_prompt
    proj = _mod_linear(y, mod, w_in, cond_of_tile)
    qkv = _conv_silu(proj, 0, B_CONV_DIM, conv_w, conv_b, mp, len_prompt, len_latent)
    ab0 = B_CONV_DIM + B_V_DIM
    ab = proj[:, ab0:ab0 + 4 * B_N_HEADS].reshape(m, 4, B_N_HEADS)
    ab_col = ab.transpose(2, 0, 1)
    ab_row = ab.reshape(m // B_CHUNK, B_CHUNK, 4, B_N_HEADS).transpose(3, 0, 2, 1)
    dt_bias = dt_bias.astype(F32)
    a_log = a_log.astype(F32)
    out, st = _gdn_core(proj, qkv, ab_col, ab_row, dt_bias, a_log, norm_g, None, None, 0, n_prompt, len_prompt, True)
    out, _ = _gdn_core(proj, qkv, ab_col, ab_row, dt_bias, a_log, norm_g, state0, out, mp, n_latent, len_latent,
                       False)
    return out, st


KV_PAIR = 2
PAIR_Q = KV_PAIR * C_GROUP * C_HEAD_DIM
PAIR_KV = KV_PAIR * C_HEAD_DIM


def _softmax_pv(scores, values, sink):
    mx = sink
    for s in scores:
        mx = jnp.maximum(mx, jnp.max(s, axis=-1, keepdims=True))
    den = jnp.exp(sink - mx)
    acc = None
    for s, v in zip(scores, values):
        p = jnp.exp(s - mx)
        den = den + jnp.sum(p, axis=-1, keepdims=True)
        pv = _dot(p.astype(BF16), v)
        acc = pv if acc is None else acc + pv
    return acc / den


def _attn_ctx_kernel(q_ref, k_ref, v_ref, sink_ref, o_ref):
    pair = pl.program_id(1)
    hd = C_HEAD_DIM
    for gg in range(KV_PAIR):
        kb = k_ref[:, gg * hd:(gg + 1) * hd].astype(BF16)
        vb = v_ref[:, gg * hd:(gg + 1) * hd].astype(BF16)
        for r in range(C_GROUP):
            col = (gg * C_GROUP + r) * hd
            qb = q_ref[:, col:col + hd].astype(BF16)
            s = _dot_nt(qb, kb) * C_SCALE
            sink = sink_ref[pair * (KV_PAIR * C_GROUP) + gg * C_GROUP + r]
            o_ref[:, col:col + hd] = _softmax_pv([s], [vb], sink)


def _attn_ctx(proj, sink, n_prompt, len_prompt):
    m = proj.shape[0]
    k_block0 = C_Q_DIM // PAIR_KV
    v_block0 = (C_Q_DIM + C_KV_DIM) // PAIR_KV
    return pl.pallas_call(
        _attn_ctx_kernel,
        grid=(n_prompt, C_N_KV // KV_PAIR),
        in_specs=[
            pl.BlockSpec((len_prompt, PAIR_Q), lambda n, p: (n, p)),
            pl.BlockSpec((len_prompt, PAIR_KV), lambda n, p: (n, k_block0 + p)),
            pl.BlockSpec((len_prompt, PAIR_KV), lambda n, p: (n, v_block0 + p)),
            pl.BlockSpec(memory_space=pltpu.SMEM),
        ],
        out_specs=pl.BlockSpec((len_prompt, PAIR_Q), lambda n, p: (n, p)),
        out_shape=jax.ShapeDtypeStruct((m, C_Q_DIM), F32),
        compiler_params=_cparams(2),
        name="attn_ctx",
    )(proj, proj, proj, sink)


def _rope(x, cos, sin):
    width = x.shape[1]
    lane = lax.broadcasted_iota(jnp.int32, x.shape, 1)
    nf = C_HEAD_DIM // 4
    partner = jnp.where(lane % (2 * nf) < nf, pltpu.roll(x, width - nf, 1), pltpu.roll(x, nf, 1))
    return x * cos + partner * sin


def _attn_lat_kernel(q_ref, kp_ref, vp_ref, kc_ref, vc_ref, cq_ref, sq_ref, ck_ref, sk_ref, sink_ref, prev_ref, o_ref,
                     *, seq_len):
    del prev_ref
    pair = pl.program_id(1)
    qi = pl.program_id(2)
    hd = C_HEAD_DIM
    blk = C_BLOCK
    span = C_BLOCK + 2 * C_WINDOW
    start = pl.multiple_of(qi * blk, blk)
    q = _rope(q_ref[...], cq_ref[...], sq_ref[...])
    kw = _rope(kp_ref[pl.ds(start, span), :], ck_ref[pl.ds(start, span), :], sk_ref[pl.ds(start, span), :])
    vw = vp_ref[pl.ds(start, span), :]
    iq = lax.broadcasted_iota(jnp.int32, (blk, span), 0)
    jk = lax.broadcasted_iota(jnp.int32, (blk, span), 1)
    kpos = start - C_WINDOW + jk
    ok = jnp.logical_and(jnp.abs(iq + C_WINDOW - jk) <= C_WINDOW, jnp.logical_and(kpos >= 0, kpos < seq_len))
    for gg in range(KV_PAIR):
        kb = kw[:, gg * hd:(gg + 1) * hd].astype(BF16)
        vb = vw[:, gg * hd:(gg + 1) * hd].astype(BF16)
        kcb = kc_ref[:, gg * hd:(gg + 1) * hd].astype(BF16)
        vcb = vc_ref[:, gg * hd:(gg + 1) * hd].astype(BF16)
        for r in range(C_GROUP):
            col = (gg * C_GROUP + r) * hd
            qb = q[:, col:col + hd].astype(BF16)
            s_loc = jnp.where(ok, _dot_nt(qb, kb) * C_SCALE, -jnp.inf)
            s_ctx = _dot_nt(qb, kcb) * C_SCALE
            sink = sink_ref[pair * (KV_PAIR * C_GROUP) + gg * C_GROUP + r]
            o_ref[:, col:col + hd] = _softmax_pv([s_loc, s_ctx], [vb, vcb], sink)


def _rope_tables(seq_len):
    nf = C_HEAD_DIM // 4
    t = jnp.arange(seq_len)
    inv_freq = ROPE_BASE ** (-jnp.arange(nf, dtype=F32) / nf)
    ang_r = (t // GRID_W).astype(F32)[:, None] * inv_freq
    ang_c = (t % GRID_W).astype(F32)[:, None] * inv_freq
    cos = jnp.concatenate([jnp.cos(ang_r), jnp.cos(ang_r), jnp.cos(ang_c), jnp.cos(ang_c)], axis=1)
    sin = jnp.concatenate([-jnp.sin(ang_r), jnp.sin(ang_r), -jnp.sin(ang_c), jnp.sin(ang_c)], axis=1)
    return cos, sin


def _attn_lat(proj, cache_k, cache_v, sink, prev, row0, n_latent, len_latent):
    m = proj.shape[0]
    past = cache_k.shape[1]
    w = C_WINDOW
    k_lat = proj[row0:, C_Q_DIM:C_Q_DIM + C_KV_DIM].reshape(n_latent, len_latent, C_KV_DIM)
    v_lat = proj[row0:, C_Q_DIM + C_KV_DIM:C_Q_DIM + 2 * C_KV_DIM].reshape(n_latent, len_latent, C_KV_DIM)
    kp = jnp.pad(k_lat, ((0, 0), (w, w), (0, 0)))
    vp = jnp.pad(v_lat, ((0, 0), (w, w), (0, 0)))
    kc = cache_k.reshape(n_latent, past, C_KV_DIM).astype(F32)
    vc = cache_v.reshape(n_latent, past, C_KV_DIM).astype(F32)
    cos, sin = _rope_tables(len_latent)
    cos_q = jnp.tile(cos, (1, PAIR_Q // C_HEAD_DIM))
    sin_q = jnp.tile(sin, (1, PAIR_Q // C_HEAD_DIM))
    cos_k = jnp.pad(jnp.tile(cos, (1, KV_PAIR)), ((w, w), (0, 0)))
    sin_k = jnp.pad(jnp.tile(sin, (1, KV_PAIR)), ((w, w), (0, 0)))
    qb0 = row0 // C_BLOCK
    nqb = len_latent // C_BLOCK
    padded = len_latent + 2 * w
    return pl.pallas_call(
        functools.partial(_attn_lat_kernel, seq_len=len_latent),
        grid=(n_latent, C_N_KV // KV_PAIR, nqb),
        in_specs=[
            pl.BlockSpec((C_BLOCK, PAIR_Q), lambda n, p, i: (qb0 + n * nqb + i, p)),
            pl.BlockSpec((None, padded, PAIR_KV), lambda n, p, i: (n, 0, p)),
            pl.BlockSpec((None, padded, PAIR_KV), lambda n, p, i: (n, 0, p)),
            pl.BlockSpec((None, past, PAIR_KV), lambda n, p, i: (n, 0, p)),
            pl.BlockSpec((None, past, PAIR_KV), lambda n, p, i: (n, 0, p)),
            pl.BlockSpec((C_BLOCK, PAIR_Q), lambda n, p, i: (i, 0)),
            pl.BlockSpec((C_BLOCK, PAIR_Q), lambda n, p, i: (i, 0)),
            pl.BlockSpec((padded, PAIR_KV), lambda n, p, i: (0, 0)),
            pl.BlockSpec((padded, PAIR_KV), lambda n, p, i: (0, 0)),
            pl.BlockSpec(memory_space=pltpu.SMEM),
            pl.BlockSpec(memory_space=pl.ANY),
        ],
        out_specs=pl.BlockSpec((C_BLOCK, PAIR_Q), lambda n, p, i: (qb0 + n * nqb + i, p)),
        out_shape=jax.ShapeDtypeStruct((m, C_Q_DIM), F32),
        input_output_aliases={10: 0},
        compiler_params=_cparams(3),
        name="attn_lat",
    )(proj, kp, vp, kc, vc, cos_q, sin_q, cos_k, sin_k, sink, prev)


def _attn_mixer(y, mod, cond_of_tile, w_in, sink, cache_k, cache_v, n_prompt, len_prompt, n_latent, len_latent):
    m = y.shape[0]
    mp = n_prompt * len_prompt
    proj = _mod_linear(y, mod, w_in, cond_of_tile)
    sink = sink.astype(F32)
    out = _attn_ctx(proj, sink, n_prompt, len_prompt)
    out = _attn_lat(proj, cache_k, cache_v, sink, out, mp, n_latent, len_latent)
    new_k = proj[:mp, C_Q_DIM:C_Q_DIM + C_KV_DIM].reshape(n_prompt, len_prompt, C_N_KV, C_HEAD_DIM)
    new_v = proj[:mp, C_Q_DIM + C_KV_DIM:C_Q_DIM + 2 * C_KV_DIM].reshape(n_prompt, len_prompt, C_N_KV, C_HEAD_DIM)
    return out, new_k, new_v


def kernel(x_prompt, x_sample, state_ssd, state_delta, cache_k, cache_v, c, c_ctx, w_mod, b_mod, ln_g, ln_b, ffn_w_gate, ffn_w_up, ffn_w_down, ssd_w_in, ssd_conv_w, ssd_conv_b, ssd_dt_bias, ssd_a_log, ssd_d, ssd_norm, ssd_w_out, gdn_w_in, gdn_conv_w, gdn_conv_b, gdn_dt_bias, gdn_a_log, gdn_norm, gdn_w_out, attn_w_in, attn_sink, attn_w_out):
    n_prompt, len_prompt, d = x_prompt.shape
    n_latent, len_latent, _ = x_sample.shape
    mp = n_prompt * len_prompt
    assert d == D_MODEL and n_latent + 1 <= N_COND
    assert len_prompt % ROW_TILE == 0 or ROW_TILE % len_prompt == 0
    assert mp % ROW_TILE == 0 and len_latent % ROW_TILE == 0
    assert mp % CONV_ROWS == 0 and len_latent % CONV_ROWS == 0 and CONV_ROWS % len_prompt == 0
    assert mp % len_latent == 0 and len_latent % GRID_W == 0

    n_prompt_tiles = mp // ROW_TILE
    tiles_per_request = len_latent // ROW_TILE
    cond_of_tile = functools.partial(_cond_index, n_prompt_tiles=n_prompt_tiles, tiles_per_request=tiles_per_request)

    cond = jnp.concatenate([c_ctx[None].astype(F32), c.astype(F32),
                            jnp.zeros((N_COND - 1 - n_latent, d), F32)], axis=0)
    mods = _adaln(cond, w_mod, b_mod).reshape(DEPTH, N_COND, N_MOD, d)

    y = jnp.concatenate([x_prompt.reshape(mp, d), x_sample.reshape(n_latent * len_latent, d)], axis=0)
    seqs = (n_prompt, len_prompt, n_latent, len_latent)
    ssd_states, gdn_states, k_list, v_list = [], [], [], []
    for i in range(DEPTH):
        mod = mods[i]
        y = _ffn_half(y, mod, 0, ln_g[i, 0], ln_b[i, 0], ffn_w_gate[i, 0], ffn_w_up[i, 0], ffn_w_down[i, 0],
                      cond_of_tile)
        kind, j = i % 3, i // 3
        if kind == 0:
            a, st = _ssd_mixer(y, mod, cond_of_tile, ssd_w_in[j], ssd_conv_w[j], ssd_conv_b[j], ssd_dt_bias[j],
                               ssd_a_log[j], ssd_d[j], state_ssd[:, j], *seqs)
            ssd_states.append(st)
            y = _out_proj_ln(a, y, mod, ssd_w_out[j], ln_g[i, 1], ln_b[i, 1], cond_of_tile, rms_g=ssd_norm[j])
        elif kind == 1:
            a, st = _gdn_mixer(y, mod, cond_of_tile, gdn_w_in[j], gdn_conv_w[j], gdn_conv_b[j], gdn_dt_bias[j],
                               gdn_a_log[j], gdn_norm[j], state_delta[:, j], *seqs)
            gdn_states.append(st)
            y = _out_proj_ln(a, y, mod, gdn_w_out[j], ln_g[i, 1], ln_b[i, 1], cond_of_tile)
        else:
            a, kc, vc = _attn_mixer(y, mod, cond_of_tile, attn_w_in[j], attn_sink[j], cache_k[:, j], cache_v[:, j],
                                    *seqs)
            k_list.append(kc)
            v_list.append(vc)
            y = _out_proj_ln(a, y, mod, attn_w_out[j], ln_g[i, 1], ln_b[i, 1], cond_of_tile)
        y = _ffn_half(y, mod, 2, ln_g[i, 2], ln_b[i, 2], ffn_w_gate[i, 1], ffn_w_up[i, 1], ffn_w_down[i, 1],
                      cond_of_tile)

    y_prompt = y[:mp].reshape(n_prompt, len_prompt, d)
    y_sample = y[mp:].reshape(n_latent, len_latent, d)
    return (y_prompt, y_sample, jnp.stack(ssd_states, axis=1), jnp.stack(gdn_states, axis=1),
            jnp.stack(k_list, axis=1), jnp.stack(v_list, axis=1))
```

```python
import functools
import math

import jax
import jax.numpy as jnp
from jax import lax
from jax.experimental import pallas as pl
from jax.experimental.pallas import tpu as pltpu

F32 = jnp.float32
BF16 = jnp.bfloat16

D_MODEL = 1024
DEPTH = 4
GRID_W = 64
N_MOD = 9
D_FF = 2816
DEEPNORM_ALPHA = (2 * DEPTH) ** 0.25
LN_EPS = 1e-5
RMS_EPS = 1e-6
FFN_RES = 0.5

A_D_INNER = 2 * D_MODEL
A_HEAD_DIM = 64
A_N_HEADS = A_D_INNER // A_HEAD_DIM
A_N_GROUPS = 4
A_HEADS_PER_GROUP = A_N_HEADS // A_N_GROUPS
A_GROUP_DIM = A_HEADS_PER_GROUP * A_HEAD_DIM
A_D_STATE = 128
A_CHUNK = 128
A_CONV_DIM = A_D_INNER + 2 * A_N_GROUPS * A_D_STATE

B_N_HEADS = 8
B_HEAD_K = 128
B_HEAD_V = 256
B_CHUNK = 64
B_QK_DIM = B_N_HEADS * B_HEAD_K
B_V_DIM = B_N_HEADS * B_HEAD_V
B_CONV_DIM = 2 * B_QK_DIM + B_V_DIM
GDN_SOLVE_BLOCK = 16
GDN_PREP_UNROLL = 4

C_N_HEADS = 16
C_N_KV = 4
C_GROUP = C_N_HEADS // C_N_KV
C_HEAD_DIM = 64
C_WINDOW = 128
C_BLOCK = 128
C_Q_DIM = C_N_HEADS * C_HEAD_DIM
C_KV_DIM = C_N_KV * C_HEAD_DIM
C_SCALE = C_HEAD_DIM ** -0.5
ROPE_BASE = 10000.0

N_COND = 16
ROW_TILE = 512
PROJ_TILE_MAX = 3328
PROJ_ALIGN = 256
FF_TILE = 1408
CONV_ROWS = 2048
CONV_COLS = 256
VMEM_LIMIT = 56 * 1024 * 1024


def _cparams(n_axes):
    return pltpu.CompilerParams(dimension_semantics=("arbitrary",) * n_axes, vmem_limit_bytes=VMEM_LIMIT)


def _dot(a, b, precision=None):
    return lax.dot_general(a, b, (((1,), (0,)), ((), ())), precision=precision, preferred_element_type=F32)


def _dot_nt(a, b, precision=None):
    return lax.dot_general(a, b, (((1,), (1,)), ((), ())), precision=precision, preferred_element_type=F32)


def _dot_tn(a, b, precision=None):
    return lax.dot_general(a, b, (((0,), (0,)), ((), ())), precision=precision, preferred_element_type=F32)


def _split3(x):
    hi = x.astype(BF16)
    rest = x - hi.astype(F32)
    mid = rest.astype(BF16)
    lo = (rest - mid.astype(F32)).astype(BF16)
    return [hi, mid, lo]


def _sigmoid(x):
    return 1.0 / (1.0 + jnp.exp(-x))


def _silu(x):
    return x * _sigmoid(x)


def _softplus(x):
    return jnp.maximum(x, 0.0) + jnp.log(1.0 + jnp.exp(-jnp.abs(x)))


def _layer_norm(t, g, b):
    mu = jnp.mean(t, axis=-1, keepdims=True)
    tc = t - mu
    var = jnp.mean(tc * tc, axis=-1, keepdims=True)
    return tc * lax.rsqrt(var + LN_EPS) * g + b


def _cond_index(i, n_prompt_tiles, tiles_per_request):
    return jnp.where(i < n_prompt_tiles, 0, 1 + (jnp.maximum(i - n_prompt_tiles, 0)) // tiles_per_request)


def _adaln_kernel(c_ref, w_ref, b_ref, o_ref):
    c = c_ref[...]
    h = _silu(c).astype(BF16)
    o_ref[...] = _dot(h, w_ref[...].astype(BF16)) + b_ref[...]


def _adaln(cond, w_mod, b_mod):
    n_out = N_MOD * D_MODEL
    tn = D_MODEL
    return pl.pallas_call(
        _adaln_kernel,
        grid=(DEPTH, n_out // tn),
        in_specs=[
            pl.BlockSpec((N_COND, D_MODEL), lambda l, j: (0, 0)),
            pl.BlockSpec((None, D_MODEL, tn), lambda l, j: (l, 0, j)),
            pl.BlockSpec((None, 1, tn), lambda l, j: (l, 0, j)),
        ],
        out_specs=pl.BlockSpec((None, N_COND, tn), lambda l, j: (l, 0, j)),
        out_shape=jax.ShapeDtypeStruct((DEPTH, N_COND, n_out), F32),
        compiler_params=_cparams(2),
        name="adaln",
    )(cond, w_mod, b_mod.reshape(DEPTH, 1, n_out))


def _ffn_kernel(y_ref, mod_ref, g_ref, b_ref, wg_ref, wu_ref, wd_ref, o_ref, h_scr, acc_scr, *, s, n_ff):
    j = pl.program_id(1)

    @pl.when(j == 0)
    def _():
        shift = mod_ref[pl.ds(3 * s, 1), :]
        scale = mod_ref[pl.ds(3 * s + 1, 1), :]
        h_scr[...] = (y_ref[...] * (1.0 + scale) + shift).astype(BF16)
        acc_scr[...] = jnp.zeros_like(acc_scr)

    h = h_scr[...]
    a = _dot(h, wg_ref[...])
    u = _dot(h, wu_ref[...])
    f = (_silu(a) * u).astype(BF16)
    acc_scr[...] += _dot(f, wd_ref[...])

    @pl.when(j == n_ff - 1)
    def _():
        gate = mod_ref[pl.ds(3 * s + 2, 1), :]
        t = DEEPNORM_ALPHA * y_ref[...] + (FFN_RES * gate) * acc_scr[...]
        o_ref[...] = _layer_norm(t, g_ref[...], b_ref[...])


def _ffn_half(y, mod, s, g, b, w_gate, w_up, w_down, cond_of_tile):
    m = y.shape[0]
    n_ff = D_FF // FF_TILE
    return pl.pallas_call(
        functools.partial(_ffn_kernel, s=s, n_ff=n_ff),
        grid=(m // ROW_TILE, n_ff),
        in_specs=[
            pl.BlockSpec((ROW_TILE, D_MODEL), lambda i, j: (i, 0)),
            pl.BlockSpec((None, N_MOD, D_MODEL), lambda i, j: (cond_of_tile(i), 0, 0)),
            pl.BlockSpec((1, D_MODEL), lambda i, j: (0, 0)),
            pl.BlockSpec((1, D_MODEL), lambda i, j: (0, 0)),
            pl.BlockSpec((D_MODEL, FF_TILE), lambda i, j: (0, j)),
            pl.BlockSpec((D_MODEL, FF_TILE), lambda i, j: (0, j)),
            pl.BlockSpec((FF_TILE, D_MODEL), lambda i, j: (j, 0)),
        ],
        out_specs=pl.BlockSpec((ROW_TILE, D_MODEL), lambda i, j: (i, 0)),
        out_shape=jax.ShapeDtypeStruct((m, D_MODEL), F32),
        scratch_shapes=[pltpu.VMEM((ROW_TILE, D_MODEL), BF16), pltpu.VMEM((ROW_TILE, D_MODEL), F32)],
        compiler_params=_cparams(2),
        name="ffn_half",
    )(y, mod, g.reshape(1, -1), b.reshape(1, -1), w_gate.astype(BF16), w_up.astype(BF16), w_down.astype(BF16))


def _modlin_kernel(y_ref, mod_ref, w_ref, o_ref):
    shift = mod_ref[pl.ds(3, 1), :]
    scale = mod_ref[pl.ds(4, 1), :]
    h = (y_ref[...] * (1.0 + scale) + shift).astype(BF16)
    o_ref[...] = _dot(h, w_ref[...])


def _mod_linear(y, mod, w, cond_of_tile):
    m = y.shape[0]
    n = w.shape[1]
    n_tiles = -(-n // PROJ_TILE_MAX)
    tn = -(-n // (n_tiles * PROJ_ALIGN)) * PROJ_ALIGN
    n_pad = n_tiles * tn
    w = jnp.pad(w.astype(BF16), ((0, 0), (0, n_pad - n)))
    return pl.pallas_call(
        _modlin_kernel,
        grid=(m // ROW_TILE, n_tiles),
        in_specs=[
            pl.BlockSpec((ROW_TILE, D_MODEL), lambda i, j: (i, 0)),
            pl.BlockSpec((None, N_MOD, D_MODEL), lambda i, j: (cond_of_tile(i), 0, 0)),
            pl.BlockSpec((D_MODEL, tn), lambda i, j: (0, j)),
        ],
        out_specs=pl.BlockSpec((ROW_TILE, tn), lambda i, j: (i, j)),
        out_shape=jax.ShapeDtypeStruct((m, n_pad), F32),
        compiler_params=_cparams(2),
        name="mod_linear",
    )(y, mod, w)


def _outproj_kernel(a_ref, y_ref, mod_ref, w_ref, g_ref, b_ref, *rest, rms):
    if rms:
        ng_ref, o_ref = rest
    else:
        (o_ref,) = rest
    a = a_ref[...]
    if rms:
        a = a * lax.rsqrt(jnp.mean(a * a, axis=-1, keepdims=True) + RMS_EPS) * ng_ref[...]
    mix = _dot(a.astype(BF16), w_ref[...])
    gate = mod_ref[pl.ds(5, 1), :]
    t = DEEPNORM_ALPHA * y_ref[...] + gate * mix
    o_ref[...] = _layer_norm(t, g_ref[...], b_ref[...])


def _out_proj_ln(a, y, mod, w, g, b, cond_of_tile, rms_g=None):
    m, k = a.shape
    in_specs = [
        pl.BlockSpec((ROW_TILE, k), lambda i: (i, 0)),
        pl.BlockSpec((ROW_TILE, D_MODEL), lambda i: (i, 0)),
        pl.BlockSpec((None, N_MOD, D_MODEL), lambda i: (cond_of_tile(i), 0, 0)),
        pl.BlockSpec((k, D_MODEL), lambda i: (0, 0)),
        pl.BlockSpec((1, D_MODEL), lambda i: (0, 0)),
        pl.BlockSpec((1, D_MODEL), lambda i: (0, 0)),
    ]
    args = [a, y, mod, w.astype(BF16), g.reshape(1, -1), b.reshape(1, -1)]
    if rms_g is not None:
        in_specs.append(pl.BlockSpec((1, k), lambda i: (0, 0)))
        args.append(rms_g.reshape(1, -1))
    return pl.pallas_call(
        functools.partial(_outproj_kernel, rms=rms_g is not None),
        grid=(m // ROW_TILE,),
        in_specs=in_specs,
        out_specs=pl.BlockSpec((ROW_TILE, D_MODEL), lambda i: (i, 0)),
        out_shape=jax.ShapeDtypeStruct((m, D_MODEL), F32),
        compiler_params=_cparams(1),
        name="out_proj_ln",
    )(*args)


def _conv_kernel(x_ref, w_ref, b_ref, o_ref, *, n_prompt_tiles, len_prompt, len_latent):
    is_prompt = pl.program_id(0) < n_prompt_tiles
    seq_len = jnp.where(is_prompt, len_prompt, len_latent)
    x = x_ref[...]
    rows = x.shape[0]
    row = lax.broadcasted_iota(jnp.int32, (rows, 1), 0)
    pos = jnp.where(is_prompt, row % len_prompt, row % len_latent)
    acc = b_ref[...] + w_ref[pl.ds(2, 1), :] * x
    for k in (0, 1, 3, 4):
        d = k - 2
        shifted = pltpu.roll(x, (rows - d) % rows, 0)
        valid = jnp.logical_and(pos + d >= 0, pos + d < seq_len)
        acc = acc + w_ref[pl.ds(k, 1), :] * jnp.where(valid, shifted, 0.0)
    o_ref[...] = _silu(acc)


def _conv_silu(proj, col0, width, conv_w, conv_b, n_prompt_rows, len_prompt, len_latent):
    m = proj.shape[0]
    c0 = col0 // CONV_COLS
    return pl.pallas_call(
        functools.partial(_conv_kernel, n_prompt_tiles=n_prompt_rows // CONV_ROWS,
                          len_prompt=len_prompt, len_latent=len_latent),
        grid=(m // CONV_ROWS, width // CONV_COLS),
        in_specs=[
            pl.BlockSpec((CONV_ROWS, CONV_COLS), lambda i, j: (i, c0 + j)),
            pl.BlockSpec((5, CONV_COLS), lambda i, j: (0, j)),
            pl.BlockSpec((1, CONV_COLS), lambda i, j: (0, j)),
        ],
        out_specs=pl.BlockSpec((CONV_ROWS, CONV_COLS), lambda i, j: (i, j)),
        out_shape=jax.ShapeDtypeStruct((m, width), F32),
        compiler_params=_cparams(2),
        name="conv_silu",
    )(proj, conv_w, conv_b.reshape(1, -1))


def _ssd_kernel(*refs, seq_len, has_h0, emit_state):
    x_ref, z_ref, b_ref, c_ref, dt_ref, dtt_ref, bias_ref, biast_ref, alog_ref, alogt_ref, dskip_ref = refs[:11]
    pos = 11
    h0_ref = None
    if has_h0:
        h0_ref = refs[pos]
        pos += 2
    y_ref = refs[pos]
    pos += 1
    hfin_ref = None
    if emit_state:
        hfin_ref = refs[pos]
        pos += 1
    s_scr = refs[pos]

    q = A_CHUNK
    nc = seq_len // q
    hpg = A_HEADS_PER_GROUP
    hd = A_HEAD_DIM
    ii = lax.broadcasted_iota(jnp.int32, (q, q), 0)
    jj = lax.broadcasted_iota(jnp.int32, (q, q), 1)
    lower = ii >= jj
    upper = ii <= jj
    t_lower = jnp.where(lower, 1.0, 0.0).astype(BF16)
    t_upper = jnp.where(upper, 1.0, 0.0).astype(BF16)
    t_lower_k = jnp.concatenate([t_lower] * 3, axis=1)
    t_upper_k = jnp.concatenate([t_upper] * 3, axis=1)
    t_lower_r = jnp.concatenate([t_lower] * 3, axis=0)
    t_upper_r = jnp.concatenate([t_upper] * 3, axis=0)
    er = lax.broadcasted_iota(jnp.int32, (3 * hpg, hpg * hd), 0)
    ec = lax.broadcasted_iota(jnp.int32, (3 * hpg, hpg * hd), 1)
    expand = jnp.where(ec // hd == er % hpg, 1.0, 0.0).astype(BF16)

    def widen(v):
        return _dot(jnp.concatenate(_split3(v), axis=1), expand)

    y_ref[...] = dskip_ref[...] * x_ref[...]

    par = []
    for d in range(2):
        par.append(dict(
            bias=bias_ref[:, d * hpg:(d + 1) * hpg],
            a_neg=-jnp.exp(alog_ref[:, d * hpg:(d + 1) * hpg]),
            bias_t=biast_ref[d * hpg:(d + 1) * hpg, :],
            a_neg_t=-jnp.exp(alogt_ref[d * hpg:(d + 1) * hpg, :])))
        if has_h0:
            s_scr[d] = h0_ref[d]
        else:
            s_scr[d] = jnp.zeros((A_D_STATE, hpg * hd), F32)

    def body(ci, carry):
        for d in range(2):
            p = par[d]
            c = ci if d == 0 else nc - 1 - ci
            r0 = pl.multiple_of(c * q, q)
            dtc = _softplus(dt_ref[pl.ds(r0, q), d * hpg:(d + 1) * hpg] + p["bias"])
            dta = jnp.concatenate(_split3(dtc * p["a_neg"]), axis=0)
            dtc_t = _softplus(dtt_ref[c][d * hpg:(d + 1) * hpg, :] + p["bias_t"])
            dta_t = jnp.concatenate(_split3(dtc_t * p["a_neg_t"]), axis=1)
            if d == 0:
                acs = _dot(t_lower_k, dta)
                acs_t = _dot(dta_t, t_upper_r)
                mask = lower
            else:
                acs = _dot(t_upper_k, dta)
                acs_t = _dot(dta_t, t_lower_r)
                mask = upper
            xc = x_ref[pl.ds(r0, q), :]
            bb = b_ref[pl.ds(r0, q), :].astype(BF16)
            cc = c_ref[pl.ds(r0, q), :].astype(BF16)
            cb = _dot_nt(cc, bb)
            dt_w = widen(dtc)
            acs_w = widen(acs)
            tot_w = acs_w[q - 1:q, :] if d == 0 else acs_w[0:1, :]
            xdt = xc * dt_w
            xdt_b = xdt.astype(BF16)
            s_old = s_scr[d]
            y_off = jnp.exp(acs_w) * _dot(cc, s_old.astype(BF16))
            y_diag = []
            for r in range(hpg):
                seg = acs[:, r:r + 1] - acs_t[r:r + 1, :]
                lmat = jnp.exp(jnp.where(mask, seg, -jnp.inf))
                mm = (cb * lmat).astype(BF16)
                y_diag.append(_dot(mm, xdt_b[:, r * hd:(r + 1) * hd]))
            y_ref[pl.ds(r0, q), :] += jnp.concatenate(y_diag, axis=1) + y_off
            xs = (xdt * jnp.exp(tot_w - acs_w)).astype(BF16)
            s_scr[d] = s_old * jnp.exp(tot_w) + _dot_tn(bb, xs)
        return carry

    lax.fori_loop(0, nc, body, 0)
    if emit_state:
        for d in range(2):
            hfin_ref[d] = s_scr[d]

    y_ref[...] = y_ref[...] * _silu(z_ref[...])


def _ssd_core(proj, xbc, dt_col, dt_row, params, h0, prev, row0, n_seq, seq_len, emit_state):
    m = proj.shape[0]
    gd = A_GROUP_DIM
    ns = A_D_STATE
    hpg = A_HEADS_PER_GROUP
    sb = row0 // seq_len
    cb0 = row0 // A_CHUNK
    nc = seq_len // A_CHUNK
    x_blocks = A_D_INNER // gd
    bias_c, bias_r, alog_c, alog_r, dskip = params
    in_specs = [
        pl.BlockSpec((seq_len, gd), lambda n, g: (sb + n, g)),
        pl.BlockSpec((seq_len, gd), lambda n, g: (sb + n, g)),
        pl.BlockSpec((seq_len, ns), lambda n, g: (sb + n, x_blocks * (gd // ns) + g)),
        pl.BlockSpec((seq_len, ns), lambda n, g: (sb + n, x_blocks * (gd // ns) + A_N_GROUPS + g)),
        pl.BlockSpec((None, seq_len, 2 * hpg), lambda n, g: (g, sb + n, 0)),
        pl.BlockSpec((None, nc, 2 * hpg, A_CHUNK), lambda n, g: (g, sb + n, 0, 0)),
        pl.BlockSpec((None, 1, 2 * hpg), lambda n, g: (g, 0, 0)),
        pl.BlockSpec((None, 2 * hpg, 1), lambda n, g: (g, 0, 0)),
        pl.BlockSpec((None, 1, 2 * hpg), lambda n, g: (g, 0, 0)),
        pl.BlockSpec((None, 2 * hpg, 1), lambda n, g: (g, 0, 0)),
        pl.BlockSpec((None, 1, gd), lambda n, g: (g, 0, 0)),
    ]
    args = [xbc, proj, xbc, xbc, dt_col, dt_row, bias_c, bias_r, alog_c, alog_r, dskip]
    aliases = {}
    if h0 is not None:
        in_specs += [pl.BlockSpec((None, 2, None, ns, gd), lambda n, g: (n, 0, g, 0, 0)),
                     pl.BlockSpec(memory_space=pl.ANY)]
        args += [h0, prev]
        aliases = {len(args) - 1: 0}
    out_specs = [pl.BlockSpec((seq_len, gd), lambda n, g: (sb + n, g))]
    out_shape = [jax.ShapeDtypeStruct((m, A_D_INNER), F32)]
    if emit_state:
        out_specs.append(pl.BlockSpec((None, 2, None, ns, gd), lambda n, g: (n, 0, g, 0, 0)))
        out_shape.append(jax.ShapeDtypeStruct((n_seq, 2, A_N_GROUPS, ns, gd), F32))
    res = pl.pallas_call(
        functools.partial(_ssd_kernel, seq_len=seq_len, has_h0=h0 is not None, emit_state=emit_state),
        grid=(n_seq, A_N_GROUPS),
        in_specs=in_specs,
        out_specs=out_specs,
        out_shape=out_shape,
        scratch_shapes=[pltpu.VMEM((2, ns, gd), F32)],
        input_output_aliases=aliases,
        compiler_params=_cparams(2),
        name="ssd_core",
    )(*args)
    return res if emit_state else (res[0], None)


def _ssd_mixer(y, mod, cond_of_tile, w_in, conv_w, conv_b, dt_bias, a_log, d_skip, state0, n_prompt, len_prompt,
               n_latent, len_latent):
    m = y.shape[0]
    mp = n_prompt * len_prompt
    proj = _mod_linear(y, mod, w_in, cond_of_tile)
    xbc = _conv_silu(proj, A_D_INNER, A_CONV_DIM, conv_w, conv_b, mp, len_prompt, len_latent)
    hpg = A_HEADS_PER_GROUP
    dt0 = A_D_INNER + A_CONV_DIM
    dt_raw = proj[:, dt0:dt0 + 2 * A_N_HEADS]
    dt_col = dt_raw.reshape(m, 2, A_N_GROUPS, hpg).transpose(2, 0, 1, 3).reshape(A_N_GROUPS, m, 2 * hpg)
    dt_row = dt_raw.reshape(m // A_CHUNK, A_CHUNK, 2, A_N_GROUPS, hpg).transpose(3, 0, 2, 4, 1)
    dt_row = dt_row.reshape(A_N_GROUPS, m // A_CHUNK, 2 * hpg, A_CHUNK)

    def per_group(p):
        return p.reshape(2, A_N_GROUPS, hpg).transpose(1, 0, 2).reshape(A_N_GROUPS, 2 * hpg)

    bias_g = per_group(dt_bias.astype(F32))
    alog_g = per_group(a_log.astype(F32))
    dskip = jnp.repeat(d_skip.astype(F32), A_HEAD_DIM).reshape(A_N_GROUPS, 1, A_GROUP_DIM)
    params = (bias_g[:, None, :], bias_g[:, :, None], alog_g[:, None, :], alog_g[:, :, None], dskip)
    h0 = state0.reshape(n_latent, 2, A_N_GROUPS, hpg, A_HEAD_DIM, A_D_STATE).transpose(0, 1, 2, 5, 3, 4)
    h0 = h0.reshape(n_latent, 2, A_N_GROUPS, A_D_STATE, A_GROUP_DIM)
    out, st = _ssd_core(proj, xbc, dt_col, dt_row, params, None, None, 0, n_prompt, len_prompt, True)
    out, _ = _ssd_core(proj, xbc, dt_col, dt_row, params, h0, out, mp, n_latent, len_latent, False)
    st = st.reshape(n_prompt, 2, A_N_GROUPS, A_D_STATE, hpg, A_HEAD_DIM).transpose(0, 1, 2, 4, 5, 3)
    return out, st.reshape(n_prompt, 2, A_N_HEADS, A_HEAD_DIM, A_D_STATE)


def _gdn_kernel(*refs, seq_len, has_s0, emit_state):
    q_ref, k_ref, v_ref, z_ref, ab_ref, abt_ref, dtb_ref, alog_ref, ng_ref = refs[:9]
    pos = 9
    s0_ref = None
    if has_s0:
        s0_ref = refs[pos]
        pos += 2
    o_ref = refs[pos]
    pos += 1
    sfin_ref = None
    if emit_state:
        sfin_ref = refs[pos]
        pos += 1
    gq_scr, h_scr, et_scr, s_scr = refs[pos:pos + 4]

    h = pl.program_id(1)
    ch = B_CHUNK
    nc = seq_len // ch
    ii = lax.broadcasted_iota(jnp.int32, (ch, ch), 0)
    jj = lax.broadcasted_iota(jnp.int32, (ch, ch), 1)
    same_block = (ii // GDN_SOLVE_BLOCK) == (jj // GDN_SOLVE_BLOCK)
    dirs = []
    for d in range(2):
        dirs.append(dict(
            dt_bias=dtb_ref[d, h], a_neg=-jnp.exp(alog_ref[d, h]),
            incl=(ii >= jj) if d == 0 else (ii <= jj),
            incl_t=(ii <= jj) if d == 0 else (ii >= jj),
            strict=(ii > jj) if d == 0 else (ii < jj)))

    def mm(a, b):
        return _dot(a.astype(BF16), b.astype(BF16))

    def compose(a, b):
        return a + b + mm(a, b)

    n_prep = min(GDN_PREP_UNROLL, nc)

    def prepare(grp, carry):
        chains = []
        for j in range(n_prep):
            c = grp * n_prep + j
            rows = pl.ds(pl.multiple_of(c * ch, ch), ch)
            qc = q_ref[rows, :]
            kc = k_ref[rows, :]
            vc = v_ref[rows, :]
            qn = qc * lax.rsqrt(jnp.sum(qc * qc, axis=-1, keepdims=True) + RMS_EPS) * (B_HEAD_K ** -0.5)
            kn = kc * lax.rsqrt(jnp.sum(kc * kc, axis=-1, keepdims=True) + RMS_EPS)
            knb = kn.astype(BF16)
            qk = _dot_nt(qn.astype(BF16), knb)
            kk = _dot_nt(knb, knb)
            ab = ab_ref[rows, :]
            abt = abt_ref[c]
            for d in range(2):
                p = dirs[d]
                beta = _sigmoid(ab[:, d:d + 1])
                g_col = p["a_neg"] * _softplus(ab[:, 2 + d:3 + d] + p["dt_bias"])
                g_row = p["a_neg"] * _softplus(abt[2 + d:3 + d, :] + p["dt_bias"])
                gcs = jnp.sum(jnp.where(p["incl"], jnp.broadcast_to(g_row, (ch, ch)), 0.0), axis=1, keepdims=True)
                gcs_t = jnp.sum(jnp.where(p["incl_t"], jnp.broadcast_to(g_col, (ch, ch)), 0.0), axis=0,
                                keepdims=True)
                tot = gcs[ch - 1:ch, :] if d == 0 else gcs[0:1, :]
                decay = jnp.exp(jnp.where(p["incl"], gcs - gcs_t, -jnp.inf))
                a_mat = jnp.where(p["strict"], beta * kk * decay, 0.0)
                a_diag = jnp.where(same_block, a_mat, 0.0)
                e_gcs = jnp.exp(gcs)
                chains.append(dict(
                    d=d, c=c, rows=rows, a_off=a_mat - a_diag, dx=-a_diag, pw=a_diag,
                    wu=jnp.concatenate([kn * (beta * e_gcs), vc * beta], axis=1),
                    kd=kn * jnp.exp(tot - gcs), qd=qn * e_gcs, qkm=(qk * decay).astype(BF16),
                    et=jnp.broadcast_to(jnp.exp(tot), (1, B_HEAD_V))))
        for _ in range(3):
            for t in chains:
                t["pw"] = mm(t["pw"], t["pw"])
            for t in chains:
                t["dx"] = compose(t["dx"], t["pw"])
        for t in chains:
            t["n"] = t["a_off"] + mm(t["dx"], t["a_off"])
        for t in chains:
            t["nn"] = mm(t["n"], t["n"])
        for t in chains:
            t["mx"] = compose(-t["n"], t["nn"])
        for t in chains:
            t["xb"] = compose(t["mx"], t["dx"]).astype(BF16)
        for t in chains:
            t["wu"] = (t["wu"] + _dot(t["xb"], t["wu"].astype(BF16))).astype(BF16)
        for t in chains:
            t["gh"] = _dot(jnp.transpose(t["kd"]).astype(BF16), t["wu"])
            t["qo"] = _dot(t["qkm"], t["wu"])
        for t in chains:
            d, c = t["d"], t["c"]
            gq_scr[d, c, 0:B_HEAD_K, :] = t["gh"][:, :B_HEAD_K].astype(BF16)
            gq_scr[d, c, B_HEAD_K:, :] = (t["qd"] - t["qo"][:, :B_HEAD_K]).astype(BF16)
            h_scr[d, c] = t["gh"][:, B_HEAD_K:]
            et_scr[d, c] = t["et"]
        for fwd, bwd in zip(chains[0::2], chains[1::2]):
            o_ref[fwd["rows"], :] = fwd["qo"][:, B_HEAD_K:] + bwd["qo"][:, B_HEAD_K:]
        return carry

    lax.fori_loop(0, nc // n_prep, prepare, 0)

    for d in range(2):
        if has_s0:
            s_scr[d] = s0_ref[d]
        else:
            s_scr[d] = jnp.zeros((B_HEAD_K, B_HEAD_V), F32)

    def recur(ci, carry):
        for d in range(2):
            c = ci if d == 0 else nc - 1 - ci
            rows = pl.ds(pl.multiple_of(c * ch, ch), ch)
            s_old = s_scr[d]
            r = _dot(gq_scr[d, c], s_old.astype(BF16))
            s_scr[d] = s_old * et_scr[d, c] - r[:B_HEAD_K] + h_scr[d, c]
            o_ref[rows, :] += r[B_HEAD_K:]
        return carry

    lax.fori_loop(0, nc, recur, 0)
    if emit_state:
        for d in range(2):
            sfin_ref[d] = s_scr[d]

    o = o_ref[...]
    o = o * lax.rsqrt(jnp.mean(o * o, axis=-1, keepdims=True) + RMS_EPS) * ng_ref[...]
    o_ref[...] = o * _silu(z_ref[...])


def _gdn_core(proj, qkv, ab_col, ab_row, dt_bias, a_log, norm_g, s0, prev, row0, n_seq, seq_len, emit_state):
    m = proj.shape[0]
    hk, hv = B_HEAD_K, B_HEAD_V
    sb = row0 // seq_len
    nc = seq_len // B_CHUNK
    z_block0 = B_CONV_DIM // hv
    in_specs = [
        pl.BlockSpec((seq_len, hk), lambda n, h: (sb + n, h)),
        pl.BlockSpec((seq_len, hk), lambda n, h: (sb + n, B_N_HEADS + h)),
        pl.BlockSpec((seq_len, hv), lambda n, h: (sb + n, (2 * B_QK_DIM) // hv + h)),
        pl.BlockSpec((seq_len, hv), lambda n, h: (sb + n, z_block0 + h)),
        pl.BlockSpec((None, seq_len, 4), lambda n, h: (h, sb + n, 0)),
        pl.BlockSpec((None, nc, 4, B_CHUNK), lambda n, h: (h, sb + n, 0, 0)),
        pl.BlockSpec(memory_space=pltpu.SMEM),
        pl.BlockSpec(memory_space=pltpu.SMEM),
        pl.BlockSpec((1, hv), lambda n, h: (0, 0)),
    ]
    args = [qkv, qkv, qkv, proj, ab_col, ab_row, dt_bias, a_log, norm_g.reshape(1, hv)]
    aliases = {}
    if s0 is not None:
        in_specs += [pl.BlockSpec((None, 2, None, hk, hv), lambda n, h: (n, 0, h, 0, 0)),
                     pl.BlockSpec(memory_space=pl.ANY)]
        args += [s0, prev]
        aliases = {len(args) - 1: 0}
    out_specs = [pl.BlockSpec((seq_len, hv), lambda n, h: (sb + n, h))]
    out_shape = [jax.ShapeDtypeStruct((m, B_V_DIM), F32)]
    if emit_state:
        out_specs.append(pl.BlockSpec((None, 2, None, hk, hv), lambda n, h: (n, 0, h, 0, 0)))
        out_shape.append(jax.ShapeDtypeStruct((n_seq, 2, B_N_HEADS, hk, hv), F32))
    res = pl.pallas_call(
        functools.partial(_gdn_kernel, seq_len=seq_len, has_s0=s0 is not None, emit_state=emit_state),
        grid=(n_seq, B_N_HEADS),
        in_specs=in_specs,
        out_specs=out_specs,
        out_shape=out_shape,
        scratch_shapes=[
            pltpu.VMEM((2, nc, hk + B_CHUNK, hk), BF16),
            pltpu.VMEM((2, nc, hk, hv), F32),
            pltpu.VMEM((2, nc, 1, hv), F32),
            pltpu.VMEM((2, hk, hv), F32),
        ],
        input_output_aliases=aliases,
        compiler_params=_cparams(2),
        name="gdn_core",
    )(*args)
    return res if emit_state else (res[0], None)


def _gdn_mixer(y, mod, cond_of_tile, w_in, conv_w, conv_b, dt_bias, a_log, norm_g, state0, n_prompt, len_prompt,
               n_latent, len_latent):
    m = y.shape[0]
    mp = n_prompt * len_prompt
    proj = _mod_linear(y, mod, w_in, cond_of_tile)
    qkv = _conv_silu(proj, 0, B_CONV_DIM, conv_w, conv_b, mp, len_prompt, len_latent)
    ab0 = B_CONV_DIM + B_V_DIM
    ab = proj[:, ab0:ab0 + 4 * B_N_HEADS].reshape(m, 4, B_N_HEADS)
    ab_col = ab.transpose(2, 0, 1)
    ab_row = ab.reshape(m // B_CHUNK, B_CHUNK, 4, B_N_HEADS).transpose(3, 0, 2, 1)
    dt_bias = dt_bias.astype(F32)
    a_log = a_log.astype(F32)
    out, st = _gdn_core(proj, qkv, ab_col, ab_row, dt_bias, a_log, norm_g, None, None, 0, n_prompt, len_prompt, True)
    out, _ = _gdn_core(proj, qkv, ab_col, ab_row, dt_bias, a_log, norm_g, state0, out, mp, n_latent, len_latent,
                       False)
    return out, st


KV_PAIR = 2
PAIR_Q = KV_PAIR * C_GROUP * C_HEAD_DIM
PAIR_KV = KV_PAIR * C_HEAD_DIM


def _softmax_pv(scores, values, sink):
    mx = sink
    for s in scores:
        mx = jnp.maximum(mx, jnp.max(s, axis=-1, keepdims=True))
    den = jnp.exp(sink - mx)
    acc = None
    for s, v in zip(scores, values):
        p = jnp.exp(s - mx)
        den = den + jnp.sum(p, axis=-1, keepdims=True)
        pv = _dot(p.astype(BF16), v)
        acc = pv if acc is None else acc + pv
    return acc / den


def _attn_ctx_kernel(q_ref, k_ref, v_ref, sink_ref, o_ref):
    pair = pl.program_id(1)
    hd = C_HEAD_DIM
    for gg in range(KV_PAIR):
        kb = k_ref[:, gg * hd:(gg + 1) * hd].astype(BF16)
        vb = v_ref[:, gg * hd:(gg + 1) * hd].astype(BF16)
        for r in range(C_GROUP):
            col = (gg * C_GROUP + r) * hd
            qb = q_ref[:, col:col + hd].astype(BF16)
            s = _dot_nt(qb, kb) * C_SCALE
            sink = sink_ref[pair * (KV_PAIR * C_GROUP) + gg * C_GROUP + r]
            o_ref[:, col:col + hd] = _softmax_pv([s], [vb], sink)


def _attn_ctx(proj, sink, n_prompt, len_prompt):
    m = proj.shape[0]
    k_block0 = C_Q_DIM // PAIR_KV
    v_block0 = (C_Q_DIM + C_KV_DIM) // PAIR_KV
    return pl.pallas_call(
        _attn_ctx_kernel,
        grid=(n_prompt, C_N_KV // KV_PAIR),
        in_specs=[
            pl.BlockSpec((len_prompt, PAIR_Q), lambda n, p: (n, p)),
            pl.BlockSpec((len_prompt, PAIR_KV), lambda n, p: (n, k_block0 + p)),
            pl.BlockSpec((len_prompt, PAIR_KV), lambda n, p: (n, v_block0 + p)),
            pl.BlockSpec(memory_space=pltpu.SMEM),
        ],
        out_specs=pl.BlockSpec((len_prompt, PAIR_Q), lambda n, p: (n, p)),
        out_shape=jax.ShapeDtypeStruct((m, C_Q_DIM), F32),
        compiler_params=_cparams(2),
        name="attn_ctx",
    )(proj, proj, proj, sink)


def _rope(x, cos, sin):
    width = x.shape[1]
    lane = lax.broadcasted_iota(jnp.int32, x.shape, 1)
    nf = C_HEAD_DIM // 4
    partner = jnp.where(lane % (2 * nf) < nf, pltpu.roll(x, width - nf, 1), pltpu.roll(x, nf, 1))
    return x * cos + partner * sin


def _attn_lat_kernel(q_ref, kp_ref, vp_ref, kc_ref, vc_ref, cq_ref, sq_ref, ck_ref, sk_ref, sink_ref, prev_ref, o_ref,
                     *, seq_len):
    del prev_ref
    pair = pl.program_id(1)
    qi = pl.program_id(2)
    hd = C_HEAD_DIM
    blk = C_BLOCK
    span = C_BLOCK + 2 * C_WINDOW
    start = pl.multiple_of(qi * blk, blk)
    q = _rope(q_ref[...], cq_ref[...], sq_ref[...])
    kw = _rope(kp_ref[pl.ds(start, span), :], ck_ref[pl.ds(start, span), :], sk_ref[pl.ds(start, span), :])
    vw = vp_ref[pl.ds(start, span), :]
    iq = lax.broadcasted_iota(jnp.int32, (blk, span), 0)
    jk = lax.broadcasted_iota(jnp.int32, (blk, span), 1)
    kpos = start - C_WINDOW + jk
    ok = jnp.logical_and(jnp.abs(iq + C_WINDOW - jk) <= C_WINDOW, jnp.logical_and(kpos >= 0, kpos < seq_len))
    for gg in range(KV_PAIR):
        kb = kw[:, gg * hd:(gg + 1) * hd].astype(BF16)
        vb = vw[:, gg * hd:(gg + 1) * hd].astype(BF16)
        kcb = kc_ref[:, gg * hd:(gg + 1) * hd].astype(BF16)
        vcb = vc_ref[:, gg * hd:(gg + 1) * hd].astype(BF16)
        for r in range(C_GROUP):
            col = (gg * C_GROUP + r) * hd
            qb = q[:, col:col + hd].astype(BF16)
            s_loc = jnp.where(ok, _dot_nt(qb, kb) * C_SCALE, -jnp.inf)
            s_ctx = _dot_nt(qb, kcb) * C_SCALE
            sink = sink_ref[pair * (KV_PAIR * C_GROUP) + gg * C_GROUP + r]
            o_ref[:, col:col + hd] = _softmax_pv([s_loc, s_ctx], [vb, vcb], sink)


def _rope_tables(seq_len):
    nf = C_HEAD_DIM // 4
    t = jnp.arange(seq_len)
    inv_freq = ROPE_BASE ** (-jnp.arange(nf, dtype=F32) / nf)
    ang_r = (t // GRID_W).astype(F32)[:, None] * inv_freq
    ang_c = (t % GRID_W).astype(F32)[:, None] * inv_freq
    cos = jnp.concatenate([jnp.cos(ang_r), jnp.cos(ang_r), jnp.cos(ang_c), jnp.cos(ang_c)], axis=1)
    sin = jnp.concatenate([-jnp.sin(ang_r), jnp.sin(ang_r), -jnp.sin(ang_c), jnp.sin(ang_c)], axis=1)
    return cos, sin


def _attn_lat(proj, cache_k, cache_v, sink, prev, row0, n_latent, len_latent):
    m = proj.shape[0]
    past = cache_k.shape[1]
    w = C_WINDOW
    k_lat = proj[row0:, C_Q_DIM:C_Q_DIM + C_KV_DIM].reshape(n_latent, len_latent, C_KV_DIM)
    v_lat = proj[row0:, C_Q_DIM + C_KV_DIM:C_Q_DIM + 2 * C_KV_DIM].reshape(n_latent, len_latent, C_KV_DIM)
    kp = jnp.pad(k_lat, ((0, 0), (w, w), (0, 0)))
    vp = jnp.pad(v_lat, ((0, 0), (w, w), (0, 0)))
    kc = cache_k.reshape(n_latent, past, C_KV_DIM).astype(F32)
    vc = cache_v.reshape(n_latent, past, C_KV_DIM).astype(F32)
    cos, sin = _rope_tables(len_latent)
    cos_q = jnp.tile(cos, (1, PAIR_Q // C_HEAD_DIM))
    sin_q = jnp.tile(sin, (1, PAIR_Q // C_HEAD_DIM))
    cos_k = jnp.pad(jnp.tile(cos, (1, KV_PAIR)), ((w, w), (0, 0)))
    sin_k = jnp.pad(jnp.tile(sin, (1, KV_PAIR)), ((w, w), (0, 0)))
    qb0 = row0 // C_BLOCK
    nqb = len_latent // C_BLOCK
    padded = len_latent + 2 * w
    return pl.pallas_call(
        functools.partial(_attn_lat_kernel, seq_len=len_latent),
        grid=(n_latent, C_N_KV // KV_PAIR, nqb),
        in_specs=[
            pl.BlockSpec((C_BLOCK, PAIR_Q), lambda n, p, i: (qb0 + n * nqb + i, p)),
            pl.BlockSpec((None, padded, PAIR_KV), lambda n, p, i: (n, 0, p)),
            pl.BlockSpec((None, padded, PAIR_KV), lambda n, p, i: (n, 0, p)),
            pl.BlockSpec((None, past, PAIR_KV), lambda n, p, i: (n, 0, p)),
            pl.BlockSpec((None, past, PAIR_KV), lambda n, p, i: (n, 0, p)),
            pl.BlockSpec((C_BLOCK, PAIR_Q), lambda n, p, i: (i, 0)),
            pl.BlockSpec((C_BLOCK, PAIR_Q), lambda n, p, i: (i, 0)),
            pl.BlockSpec((padded, PAIR_KV), lambda n, p, i: (0, 0)),
            pl.BlockSpec((padded, PAIR_KV), lambda n, p, i: (0, 0)),
            pl.BlockSpec(memory_space=pltpu.SMEM),
            pl.BlockSpec(memory_space=pl.ANY),
        ],
        out_specs=pl.BlockSpec((C_BLOCK, PAIR_Q), lambda n, p, i: (qb0 + n * nqb + i, p)),
        out_shape=jax.ShapeDtypeStruct((m, C_Q_DIM), F32),
        input_output_aliases={10: 0},
        compiler_params=_cparams(3),
        name="attn_lat",
    )(proj, kp, vp, kc, vc, cos_q, sin_q, cos_k, sin_k, sink, prev)


def _attn_mixer(y, mod, cond_of_tile, w_in, sink, cache_k, cache_v, n_prompt, len_prompt, n_latent, len_latent):
    m = y.shape[0]
    mp = n_prompt * len_prompt
    proj = _mod_linear(y, mod, w_in, cond_of_tile)
    sink = sink.astype(F32)
    out = _attn_ctx(proj, sink, n_prompt, len_prompt)
    out = _attn_lat(proj, cache_k, cache_v, sink, out, mp, n_latent, len_latent)
    new_k = proj[:mp, C_Q_DIM:C_Q_DIM + C_KV_DIM].reshape(n_prompt, len_prompt, C_N_KV, C_HEAD_DIM)
    new_v = proj[:mp, C_Q_DIM + C_KV_DIM:C_Q_DIM + 2 * C_KV_DIM].reshape(n_prompt, len_prompt, C_N_KV, C_HEAD_DIM)
    return out, new_k, new_v


def kernel(x_prompt, x_sample, state_ssd, state_delta, cache_k, cache_v, c, c_ctx, w_mod, b_mod, ln_g, ln_b, ffn_w_gate, ffn_w_up, ffn_w_down, ssd_w_in, ssd_conv_w, ssd_conv_b, ssd_dt_bias, ssd_a_log, ssd_d, ssd_norm, ssd_w_out, gdn_w_in, gdn_conv_w, gdn_conv_b, gdn_dt_bias, gdn_a_log, gdn_norm, gdn_w_out, attn_w_in, attn_sink, attn_w_out):
    n_prompt, len_prompt, d = x_prompt.shape
    n_latent, len_latent, _ = x_sample.shape
    mp = n_prompt * len_prompt
    assert d == D_MODEL and n_latent + 1 <= N_COND
    assert len_prompt % ROW_TILE == 0 or ROW_TILE % len_prompt == 0
    assert mp % ROW_TILE == 0 and len_latent % ROW_TILE == 0
    assert mp % CONV_ROWS == 0 and len_latent % CONV_ROWS == 0 and CONV_ROWS % len_prompt == 0
    assert mp % len_latent == 0 and len_latent % GRID_W == 0

    n_prompt_tiles = mp // ROW_TILE
    tiles_per_request = len_latent // ROW_TILE
    cond_of_tile = functools.partial(_cond_index, n_prompt_tiles=n_prompt_tiles, tiles_per_request=tiles_per_request)

    cond = jnp.concatenate([c_ctx[None].astype(F32), c.astype(F32),
                            jnp.zeros((N_COND - 1 - n_latent, d), F32)], axis=0)
    mods = _adaln(cond, w_mod, b_mod).reshape(DEPTH, N_COND, N_MOD, d)

    y = jnp.concatenate([x_prompt.reshape(mp, d), x_sample.reshape(n_latent * len_latent, d)], axis=0)
    seqs = (n_prompt, len_prompt, n_latent, len_latent)
    ssd_states, gdn_states, k_list, v_list = [], [], [], []
    for i in range(DEPTH):
        mod = mods[i]
        y = _ffn_half(y, mod, 0, ln_g[i, 0], ln_b[i, 0], ffn_w_gate[i, 0], ffn_w_up[i, 0], ffn_w_down[i, 0],
                      cond_of_tile)
        kind, j = i % 3, i // 3
        if kind == 0:
            a, st = _ssd_mixer(y, mod, cond_of_tile, ssd_w_in[j], ssd_conv_w[j], ssd_conv_b[j], ssd_dt_bias[j],
                               ssd_a_log[j], ssd_d[j], state_ssd[:, j], *seqs)
            ssd_states.append(st)
            y = _out_proj_ln(a, y, mod, ssd_w_out[j], ln_g[i, 1], ln_b[i, 1], cond_of_tile, rms_g=ssd_norm[j])
        elif kind == 1:
            a, st = _gdn_mixer(y, mod, cond_of_tile, gdn_w_in[j], gdn_conv_w[j], gdn_conv_b[j], gdn_dt_bias[j],
                               gdn_a_log[j], gdn_norm[j], state_delta[:, j], *seqs)
            gdn_states.append(st)
            y = _out_proj_ln(a, y, mod, gdn_w_out[j], ln_g[i, 1], ln_b[i, 1], cond_of_tile)
        else:
            a, kc, vc = _attn_mixer(y, mod, cond_of_tile, attn_w_in[j], attn_sink[j], cache_k[:, j], cache_v[:, j],
                                    *seqs)
            k_list.append(kc)
            v_list.append(vc)
            y = _out_proj_ln(a, y, mod, attn_w_out[j], ln_g[i, 1], ln_b[i, 1], cond_of_tile)
        y = _ffn_half(y, mod, 2, ln_g[i, 2], ln_b[i, 2], ffn_w_gate[i, 1], ffn_w_up[i, 1], ffn_w_down[i, 1],
                      cond_of_tile)

    y_prompt = y[:mp].reshape(n_prompt, len_prompt, d)
    y_sample = y[mp:].reshape(n_latent, len_latent, d)
    return (y_prompt, y_sample, jnp.stack(ssd_states, axis=1), jnp.stack(gdn_states, axis=1),
            jnp.stack(k_list, axis=1), jnp.stack(v_list, axis=1))
```

```python
import functools
import math

import jax
import jax.numpy as jnp
from jax import lax
from jax.experimental import pallas as pl
from jax.experimental.pallas import tpu as pltpu

F32 = jnp.float32
BF16 = jnp.bfloat16

D_MODEL = 1024
DEPTH = 4
GRID_W = 64
N_MOD = 9
D_FF = 2816
DEEPNORM_ALPHA = (2 * DEPTH) ** 0.25
LN_EPS = 1e-5
RMS_EPS = 1e-6
FFN_RES = 0.5

A_D_INNER = 2 * D_MODEL
A_HEAD_DIM = 64
A_N_HEADS = A_D_INNER // A_HEAD_DIM
A_N_GROUPS = 4
A_HEADS_PER_GROUP = A_N_HEADS // A_N_GROUPS
A_GROUP_DIM = A_HEADS_PER_GROUP * A_HEAD_DIM
A_D_STATE = 128
A_CHUNK = 128
A_CONV_DIM = A_D_INNER + 2 * A_N_GROUPS * A_D_STATE

B_N_HEADS = 8
B_HEAD_K = 128
B_HEAD_V = 256
B_CHUNK = 64
B_QK_DIM = B_N_HEADS * B_HEAD_K
B_V_DIM = B_N_HEADS * B_HEAD_V
B_CONV_DIM = 2 * B_QK_DIM + B_V_DIM
GDN_SOLVE_BLOCK = 16
GDN_PREP_UNROLL = 4
GDN_HEADS_PER_STEP = 2

C_N_HEADS = 16
C_N_KV = 4
C_GROUP = C_N_HEADS // C_N_KV
C_HEAD_DIM = 64
C_WINDOW = 128
C_BLOCK = 128
C_Q_DIM = C_N_HEADS * C_HEAD_DIM
C_KV_DIM = C_N_KV * C_HEAD_DIM
C_SCALE = C_HEAD_DIM ** -0.5
ROPE_BASE = 10000.0

N_COND = 16
ROW_TILE = 512
PROJ_TILE_MAX = 3328
PROJ_ALIGN = 256
FF_TILE = 1408
CONV_KERNEL = 5
CONV_PAD = 16
CONV_COLS = 256
VMEM_LIMIT = 56 * 1024 * 1024


def _cparams(n_axes):
    return pltpu.CompilerParams(dimension_semantics=("arbitrary",) * n_axes, vmem_limit_bytes=VMEM_LIMIT)


def _dot(a, b, precision=None):
    return lax.dot_general(a, b, (((1,), (0,)), ((), ())), precision=precision, preferred_element_type=F32)


def _dot_nt(a, b, precision=None):
    return lax.dot_general(a, b, (((1,), (1,)), ((), ())), precision=precision, preferred_element_type=F32)


def _dot_tn(a, b, precision=None):
    return lax.dot_general(a, b, (((0,), (0,)), ((), ())), precision=precision, preferred_element_type=F32)


def _split3(x):
    hi = x.astype(BF16)
    rest = x - hi.astype(F32)
    mid = rest.astype(BF16)
    lo = (rest - mid.astype(F32)).astype(BF16)
    return [hi, mid, lo]


def _sigmoid(x):
    return 1.0 / (1.0 + jnp.exp(-x))


def _silu(x):
    return x * _sigmoid(x)


def _softplus(x):
    return jnp.maximum(x, 0.0) + jnp.log(1.0 + jnp.exp(-jnp.abs(x)))


def _layer_norm(t, g, b):
    mu = jnp.mean(t, axis=-1, keepdims=True)
    tc = t - mu
    var = jnp.mean(tc * tc, axis=-1, keepdims=True)
    return tc * lax.rsqrt(var + LN_EPS) * g + b


def _cond_index(i, n_prompt_tiles, tiles_per_request):
    return jnp.where(i < n_prompt_tiles, 0, 1 + (jnp.maximum(i - n_prompt_tiles, 0)) // tiles_per_request)


def _adaln_kernel(c_ref, w_ref, b_ref, o_ref):
    c = c_ref[...]
    h = _silu(c).astype(BF16)
    o_ref[...] = _dot(h, w_ref[...].astype(BF16)) + b_ref[...]


def _adaln(cond, w_mod, b_mod):
    n_out = N_MOD * D_MODEL
    tn = D_MODEL
    return pl.pallas_call(
        _adaln_kernel,
        grid=(DEPTH, n_out // tn),
        in_specs=[
            pl.BlockSpec((N_COND, D_MODEL), lambda l, j: (0, 0)),
            pl.BlockSpec((None, D_MODEL, tn), lambda l, j: (l, 0, j)),
            pl.BlockSpec((None, 1, tn), lambda l, j: (l, 0, j)),
        ],
        out_specs=pl.BlockSpec((None, N_COND, tn), lambda l, j: (l, 0, j)),
        out_shape=jax.ShapeDtypeStruct((DEPTH, N_COND, n_out), F32),
        compiler_params=_cparams(2),
        name="adaln",
    )(cond, w_mod, b_mod.reshape(DEPTH, 1, n_out))


def _ffn_kernel(y_ref, mod_ref, g_ref, b_ref, wg_ref, wu_ref, wd_ref, o_ref, *, s):
    y = y_ref[...]
    shift = mod_ref[pl.ds(3 * s, 1), :]
    scale = mod_ref[pl.ds(3 * s + 1, 1), :]
    gate = mod_ref[pl.ds(3 * s + 2, 1), :]
    h = (y * (1.0 + scale) + shift).astype(BF16)
    acc = None
    for c in range(D_FF // FF_TILE):
        cols = slice(c * FF_TILE, (c + 1) * FF_TILE)
        a = _dot(h, wg_ref[:, cols])
        u = _dot(h, wu_ref[:, cols])
        f = (_silu(a) * u).astype(BF16)
        part = _dot(f, wd_ref[cols, :])
        acc = part if acc is None else acc + part
    t = DEEPNORM_ALPHA * y + (FFN_RES * gate) * acc
    o_ref[...] = _layer_norm(t, g_ref[...], b_ref[...])


def _ffn_half(y, mod, s, g, b, w_gate, w_up, w_down, cond_of_tile):
    m = y.shape[0]
    resident = pl.Buffered(1)
    return pl.pallas_call(
        functools.partial(_ffn_kernel, s=s),
        grid=(m // ROW_TILE,),
        in_specs=[
            pl.BlockSpec((ROW_TILE, D_MODEL), lambda i: (i, 0)),
            pl.BlockSpec((None, N_MOD, D_MODEL), lambda i: (cond_of_tile(i), 0, 0)),
            pl.BlockSpec((1, D_MODEL), lambda i: (0, 0)),
            pl.BlockSpec((1, D_MODEL), lambda i: (0, 0)),
            pl.BlockSpec((D_MODEL, D_FF), lambda i: (0, 0), pipeline_mode=resident),
            pl.BlockSpec((D_MODEL, D_FF), lambda i: (0, 0), pipeline_mode=resident),
            pl.BlockSpec((D_FF, D_MODEL), lambda i: (0, 0), pipeline_mode=resident),
        ],
        out_specs=pl.BlockSpec((ROW_TILE, D_MODEL), lambda i: (i, 0)),
        out_shape=jax.ShapeDtypeStruct((m, D_MODEL), F32),
        compiler_params=_cparams(1),
        name="ffn_half",
    )(y, mod, g.reshape(1, -1), b.reshape(1, -1), w_gate.astype(BF16), w_up.astype(BF16), w_down.astype(BF16))


def _modlin_kernel(y_ref, mod_ref, w_ref, o_ref):
    shift = mod_ref[pl.ds(3, 1), :]
    scale = mod_ref[pl.ds(4, 1), :]
    h = (y_ref[...] * (1.0 + scale) + shift).astype(BF16)
    o_ref[...] = _dot(h, w_ref[...]).astype(o_ref.dtype)


def _mod_linear(y, mod, w, cond_of_tile):
    m = y.shape[0]
    n = w.shape[1]
    n_tiles = -(-n // PROJ_TILE_MAX)
    tn = -(-n // (n_tiles * PROJ_ALIGN)) * PROJ_ALIGN
    n_pad = n_tiles * tn
    w = jnp.pad(w.astype(BF16), ((0, 0), (0, n_pad - n)))
    return pl.pallas_call(
        _modlin_kernel,
        grid=(n_tiles, m // ROW_TILE),
        in_specs=[
            pl.BlockSpec((ROW_TILE, D_MODEL), lambda j, i: (i, 0)),
            pl.BlockSpec((None, N_MOD, D_MODEL), lambda j, i: (cond_of_tile(i), 0, 0)),
            pl.BlockSpec((D_MODEL, tn), lambda j, i: (0, j)),
        ],
        out_specs=pl.BlockSpec((ROW_TILE, tn), lambda j, i: (i, j)),
        out_shape=jax.ShapeDtypeStruct((m, n_pad), BF16),
        compiler_params=_cparams(2),
        name="mod_linear",
    )(y, mod, w)


def _outproj_kernel(a_ref, y_ref, mod_ref, w_ref, g_ref, b_ref, *rest, rms):
    if rms:
        ng_ref, o_ref = rest
    else:
        (o_ref,) = rest
    a = a_ref[...]
    if rms:
        a = a * lax.rsqrt(jnp.mean(a * a, axis=-1, keepdims=True) + RMS_EPS) * ng_ref[...]
    mix = _dot(a.astype(BF16), w_ref[...])
    gate = mod_ref[pl.ds(5, 1), :]
    t = DEEPNORM_ALPHA * y_ref[...] + gate * mix
    o_ref[...] = _layer_norm(t, g_ref[...], b_ref[...])


def _out_proj_ln(a, y, mod, w, g, b, cond_of_tile, rms_g=None):
    m, k = a.shape
    in_specs = [
        pl.BlockSpec((ROW_TILE, k), lambda i: (i, 0)),
        pl.BlockSpec((ROW_TILE, D_MODEL), lambda i: (i, 0)),
        pl.BlockSpec((None, N_MOD, D_MODEL), lambda i: (cond_of_tile(i), 0, 0)),
        pl.BlockSpec((k, D_MODEL), lambda i: (0, 0)),
        pl.BlockSpec((1, D_MODEL), lambda i: (0, 0)),
        pl.BlockSpec((1, D_MODEL), lambda i: (0, 0)),
    ]
    args = [a, y, mod, w.astype(BF16), g.reshape(1, -1), b.reshape(1, -1)]
    if rms_g is not None:
        in_specs.append(pl.BlockSpec((1, k), lambda i: (0, 0)))
        args.append(rms_g.reshape(1, -1))
    return pl.pallas_call(
        functools.partial(_outproj_kernel, rms=rms_g is not None),
        grid=(m // ROW_TILE,),
        in_specs=in_specs,
        out_specs=pl.BlockSpec((ROW_TILE, D_MODEL), lambda i: (i, 0)),
        out_shape=jax.ShapeDtypeStruct((m, D_MODEL), F32),
        compiler_params=_cparams(1),
        name="out_proj_ln",
    )(*args)


def _conv_kernel(x_ref, w_ref, b_ref, o_ref, xs_scr, *, n_prompt_tiles, len_prompt):
    rows = x_ref.shape[0]
    pad = CONV_PAD
    half = CONV_KERNEL // 2
    zeros = jnp.zeros((pad, xs_scr.shape[1]), F32)
    xs_scr[0:pad, :] = zeros
    xs_scr[pad + rows:, :] = zeros
    xs_scr[pad:pad + rows, :] = x_ref[...].astype(F32)

    def taps(start, n, valid=None):
        acc = b_ref[...]
        for k in range(CONV_KERNEL):
            xk = xs_scr[pl.ds(pad + start + k - half, n), :]
            if valid is not None:
                xk = jnp.where(valid[k], xk, 0.0)
            acc = acc + w_ref[pl.ds(k, 1), :] * xk
        return _silu(acc).astype(o_ref.dtype)

    o_ref[...] = taps(0, rows)

    @pl.when(pl.program_id(0) < n_prompt_tiles)
    def _():
        i = lax.broadcasted_iota(jnp.int32, (2 * pad, 1), 0)
        valid = [(i < pad) == (i + k - half < pad) for k in range(CONV_KERNEL)]
        for b in range(1, rows // len_prompt):
            r = b * len_prompt
            o_ref[r - pad:r + pad, :] = taps(r - pad, 2 * pad, valid)


def _conv_silu(proj, col0, width, conv_w, conv_b, n_prompt_rows, len_prompt, len_latent):
    m = proj.shape[0]
    rows = len_latent
    c0 = col0 // CONV_COLS
    return pl.pallas_call(
        functools.partial(_conv_kernel, n_prompt_tiles=n_prompt_rows // rows, len_prompt=len_prompt),
        grid=(m // rows, width // CONV_COLS),
        in_specs=[
            pl.BlockSpec((rows, CONV_COLS), lambda i, j: (i, c0 + j)),
            pl.BlockSpec((CONV_KERNEL, CONV_COLS), lambda i, j: (0, j)),
            pl.BlockSpec((1, CONV_COLS), lambda i, j: (0, j)),
        ],
        out_specs=pl.BlockSpec((rows, CONV_COLS), lambda i, j: (i, j)),
        out_shape=jax.ShapeDtypeStruct((m, width), BF16),
        scratch_shapes=[pltpu.VMEM((rows + 2 * CONV_PAD, CONV_COLS), F32)],
        compiler_params=_cparams(2),
        name="conv_silu",
    )(proj, conv_w, conv_b.reshape(1, -1))


def _ssd_kernel(*refs, seq_len, has_h0, emit_state):
    x_ref, z_ref, b_ref, c_ref, dt_ref, dtt_ref, bias_ref, biast_ref, alog_ref, alogt_ref, dskip_ref = refs[:11]
    pos = 11
    h0_ref = None
    if has_h0:
        h0_ref = refs[pos]
        pos += 2
    y_ref = refs[pos]
    pos += 1
    hfin_ref = None
    if emit_state:
        hfin_ref = refs[pos]
        pos += 1
    s_scr = refs[pos]

    q = A_CHUNK
    nc = seq_len // q
    hpg = A_HEADS_PER_GROUP
    hd = A_HEAD_DIM
    ii = lax.broadcasted_iota(jnp.int32, (q, q), 0)
    jj = lax.broadcasted_iota(jnp.int32, (q, q), 1)
    lower = ii >= jj
    upper = ii <= jj
    t_lower = jnp.where(lower, 1.0, 0.0).astype(BF16)
    t_upper = jnp.where(upper, 1.0, 0.0).astype(BF16)
    t_lower_k = jnp.concatenate([t_lower] * 3, axis=1)
    t_upper_k = jnp.concatenate([t_upper] * 3, axis=1)
    t_lower_r = jnp.concatenate([t_lower] * 3, axis=0)
    t_upper_r = jnp.concatenate([t_upper] * 3, axis=0)
    er = lax.broadcasted_iota(jnp.int32, (3 * hpg, hpg * hd), 0)
    ec = lax.broadcasted_iota(jnp.int32, (3 * hpg, hpg * hd), 1)
    expand = jnp.where(ec // hd == er % hpg, 1.0, 0.0).astype(BF16)

    first_head = lax.broadcasted_iota(jnp.int32, (q, 2 * hd), 1) < hd

    def widen(v):
        return _dot(jnp.concatenate(_split3(v), axis=1), expand)

    y_ref[...] = dskip_ref[...] * x_ref[...].astype(F32)

    par = []
    for d in range(2):
        par.append(dict(
            bias=bias_ref[:, d * hpg:(d + 1) * hpg],
            a_neg=-jnp.exp(alog_ref[:, d * hpg:(d + 1) * hpg]),
            bias_t=biast_ref[d * hpg:(d + 1) * hpg, :],
            a_neg_t=-jnp.exp(alogt_ref[d * hpg:(d + 1) * hpg, :])))
        if has_h0:
            s_scr[d] = h0_ref[d]
        else:
            s_scr[d] = jnp.zeros((A_D_STATE, hpg * hd), F32)

    def body(ci, carry):
        ts = []
        for d in range(2):
            p = par[d]
            c = ci if d == 0 else nc - 1 - ci
            r0 = pl.multiple_of(c * q, q)
            dtc = _softplus(dt_ref[pl.ds(r0, q), d * hpg:(d + 1) * hpg] + p["bias"])
            dta = jnp.concatenate(_split3(dtc * p["a_neg"]), axis=0)
            dtc_t = _softplus(dtt_ref[c][d * hpg:(d + 1) * hpg, :] + p["bias_t"])
            dta_t = jnp.concatenate(_split3(dtc_t * p["a_neg_t"]), axis=1)
            ts.append(dict(d=d, rows=pl.ds(r0, q), dtc=dtc, dta=dta, dta_t=dta_t, mask=lower if d == 0 else upper))
        for t in ts:
            if t["d"] == 0:
                t["acs"] = _dot(t_lower_k, t["dta"])
                t["acs_t"] = _dot(t["dta_t"], t_upper_r)
            else:
                t["acs"] = _dot(t_upper_k, t["dta"])
                t["acs_t"] = _dot(t["dta_t"], t_lower_r)
            t["dt_w"] = widen(t["dtc"])
        for t in ts:
            t["bb"] = b_ref[t["rows"], :].astype(BF16)
            t["cc"] = c_ref[t["rows"], :].astype(BF16)
            t["cb"] = _dot_nt(t["cc"], t["bb"])
            t["s_old"] = s_scr[t["d"]]
            t["cs"] = _dot(t["cc"], t["s_old"].astype(BF16))
            t["xdt"] = x_ref[t["rows"], :].astype(F32) * t["dt_w"]
            t["xdt_b"] = t["xdt"].astype(BF16)
        for t in ts:
            t["acs_w"] = widen(t["acs"])
            t["tot_w"] = t["acs_w"][q - 1:q, :] if t["d"] == 0 else t["acs_w"][0:1, :]
        for t in ts:
            t["y_diag"] = []
        for pr in range(hpg // 2):
            for t in ts:
                mms = []
                for r in (2 * pr, 2 * pr + 1):
                    seg = t["acs"][:, r:r + 1] - t["acs_t"][r:r + 1, :]
                    lmat = jnp.exp(jnp.where(t["mask"], seg, -jnp.inf))
                    mms.append((t["cb"] * lmat).astype(BF16))
                x_pair = t["xdt_b"][:, 2 * pr * hd:2 * (pr + 1) * hd]
                x_diag = jnp.concatenate([jnp.where(first_head, x_pair, jnp.zeros_like(x_pair)),
                                          jnp.where(first_head, jnp.zeros_like(x_pair), x_pair)], axis=0)
                t["y_diag"].append(_dot(jnp.concatenate(mms, axis=1), x_diag))
        for t in ts:
            y_off = jnp.exp(t["acs_w"]) * t["cs"]
            y_ref[t["rows"], :] += jnp.concatenate(t["y_diag"], axis=1) + y_off
            xs = (t["xdt"] * jnp.exp(t["tot_w"] - t["acs_w"])).astype(BF16)
            s_scr[t["d"]] = t["s_old"] * jnp.exp(t["tot_w"]) + _dot_tn(t["bb"], xs)
        return carry

    lax.fori_loop(0, nc, body, 0)
    if emit_state:
        for d in range(2):
            hfin_ref[d] = s_scr[d]

    y_ref[...] = y_ref[...] * _silu(z_ref[...].astype(F32))


def _ssd_core(proj, xbc, dt_col, dt_row, params, h0, prev, row0, n_seq, seq_len, emit_state):
    m = proj.shape[0]
    gd = A_GROUP_DIM
    ns = A_D_STATE
    hpg = A_HEADS_PER_GROUP
    sb = row0 // seq_len
    cb0 = row0 // A_CHUNK
    nc = seq_len // A_CHUNK
    x_blocks = A_D_INNER // gd
    bias_c, bias_r, alog_c, alog_r, dskip = params
    in_specs = [
        pl.BlockSpec((seq_len, gd), lambda n, g: (sb + n, g)),
        pl.BlockSpec((seq_len, gd), lambda n, g: (sb + n, g)),
        pl.BlockSpec((seq_len, ns), lambda n, g: (sb + n, x_blocks * (gd // ns) + g)),
        pl.BlockSpec((seq_len, ns), lambda n, g: (sb + n, x_blocks * (gd // ns) + A_N_GROUPS + g)),
        pl.BlockSpec((None, seq_len, 2 * hpg), lambda n, g: (g, sb + n, 0)),
        pl.BlockSpec((None, nc, 2 * hpg, A_CHUNK), lambda n, g: (g, sb + n, 0, 0)),
        pl.BlockSpec((None, 1, 2 * hpg), lambda n, g: (g, 0, 0)),
        pl.BlockSpec((None, 2 * hpg, 1), lambda n, g: (g, 0, 0)),
        pl.BlockSpec((None, 1, 2 * hpg), lambda n, g: (g, 0, 0)),
        pl.BlockSpec((None, 2 * hpg, 1), lambda n, g: (g, 0, 0)),
        pl.BlockSpec((None, 1, gd), lambda n, g: (g, 0, 0)),
    ]
    args = [xbc, proj, xbc, xbc, dt_col, dt_row, bias_c, bias_r, alog_c, alog_r, dskip]
    aliases = {}
    if h0 is not None:
        in_specs += [pl.BlockSpec((None, 2, None, ns, gd), lambda n, g: (n, 0, g, 0, 0)),
                     pl.BlockSpec(memory_space=pl.ANY)]
        args += [h0, prev]
        aliases = {len(args) - 1: 0}
    out_specs = [pl.BlockSpec((seq_len, gd), lambda n, g: (sb + n, g))]
    out_shape = [jax.ShapeDtypeStruct((m, A_D_INNER), F32)]
    if emit_state:
        out_specs.append(pl.BlockSpec((None, 2, None, ns, gd), lambda n, g: (n, 0, g, 0, 0)))
        out_shape.append(jax.ShapeDtypeStruct((n_seq, 2, A_N_GROUPS, ns, gd), F32))
    res = pl.pallas_call(
        functools.partial(_ssd_kernel, seq_len=seq_len, has_h0=h0 is not None, emit_state=emit_state),
        grid=(n_seq, A_N_GROUPS),
        in_specs=in_specs,
        out_specs=out_specs,
        out_shape=out_shape,
        scratch_shapes=[pltpu.VMEM((2, ns, gd), F32)],
        input_output_aliases=aliases,
        compiler_params=_cparams(2),
        name="ssd_core",
    )(*args)
    return res if emit_state else (res[0], None)


def _ssd_mixer(y, mod, cond_of_tile, w_in, conv_w, conv_b, dt_bias, a_log, d_skip, state0, n_prompt, len_prompt,
               n_latent, len_latent):
    m = y.shape[0]
    mp = n_prompt * len_prompt
    proj = _mod_linear(y, mod, w_in, cond_of_tile)
    xbc = _conv_silu(proj, A_D_INNER, A_CONV_DIM, conv_w, conv_b, mp, len_prompt, len_latent)
    hpg = A_HEADS_PER_GROUP
    dt0 = A_D_INNER + A_CONV_DIM
    dt_raw = proj[:, dt0:dt0 + 2 * A_N_HEADS].astype(F32)
    dt_col = dt_raw.reshape(m, 2, A_N_GROUPS, hpg).transpose(2, 0, 1, 3).reshape(A_N_GROUPS, m, 2 * hpg)
    dt_row = dt_raw.reshape(m // A_CHUNK, A_CHUNK, 2, A_N_GROUPS, hpg).transpose(3, 0, 2, 4, 1)
    dt_row = dt_row.reshape(A_N_GROUPS, m // A_CHUNK, 2 * hpg, A_CHUNK)

    def per_group(p):
        return p.reshape(2, A_N_GROUPS, hpg).transpose(1, 0, 2).reshape(A_N_GROUPS, 2 * hpg)

    bias_g = per_group(dt_bias.astype(F32))
    alog_g = per_group(a_log.astype(F32))
    dskip = jnp.repeat(d_skip.astype(F32), A_HEAD_DIM).reshape(A_N_GROUPS, 1, A_GROUP_DIM)
    params = (bias_g[:, None, :], bias_g[:, :, None], alog_g[:, None, :], alog_g[:, :, None], dskip)
    h0 = state0.reshape(n_latent, 2, A_N_GROUPS, hpg, A_HEAD_DIM, A_D_STATE).transpose(0, 1, 2, 5, 3, 4)
    h0 = h0.reshape(n_latent, 2, A_N_GROUPS, A_D_STATE, A_GROUP_DIM)
    out, st = _ssd_core(proj, xbc, dt_col, dt_row, params, None, None, 0, n_prompt, len_prompt, True)
    out, _ = _ssd_core(proj, xbc, dt_col, dt_row, params, h0, out, mp, n_latent, len_latent, False)
    st = st.reshape(n_prompt, 2, A_N_GROUPS, A_D_STATE, hpg, A_HEAD_DIM).transpose(0, 1, 2, 4, 5, 3)
    return out, st.reshape(n_prompt, 2, A_N_HEADS, A_HEAD_DIM, A_D_STATE)


def _gdn_kernel(*refs, seq_len, has_s0, emit_state):
    q_ref, k_ref, v_ref, z_ref, ab_ref, abt_ref, dtb_ref, alog_ref, ng_ref = refs[:9]
    pos = 9
    s0_ref = None
    if has_s0:
        s0_ref = refs[pos]
        pos += 2
    o_ref = refs[pos]
    pos += 1
    sfin_ref = None
    if emit_state:
        sfin_ref = refs[pos]
        pos += 1
    gq_scr, h_scr, et_scr, s_scr = refs[pos:pos + 4]

    hps = GDN_HEADS_PER_STEP
    hk, hv = B_HEAD_K, B_HEAD_V
    head0 = pl.program_id(1) * hps
    ch = B_CHUNK
    nc = seq_len // ch
    ii = lax.broadcasted_iota(jnp.int32, (ch, ch), 0)
    jj = lax.broadcasted_iota(jnp.int32, (ch, ch), 1)
    same_block = (ii // GDN_SOLVE_BLOCK) == (jj // GDN_SOLVE_BLOCK)
    dirs = []
    for d in range(2):
        dirs.append(dict(
            dt_bias=[dtb_ref[d, head0 + hh] for hh in range(hps)],
            a_neg=[-jnp.exp(alog_ref[d, head0 + hh]) for hh in range(hps)],
            incl=(ii >= jj) if d == 0 else (ii <= jj),
            incl_t=(ii <= jj) if d == 0 else (ii >= jj),
            strict=(ii > jj) if d == 0 else (ii < jj)))

    def mm(a, b):
        return _dot(a.astype(BF16), b.astype(BF16))

    def compose(a, b):
        return a + b + mm(a, b)

    n_prep = min(GDN_PREP_UNROLL, nc)

    def prepare(grp, carry):
        chains = []
        for j, hh in [(j, hh) for j in range(n_prep) for hh in range(hps)]:
            c = grp * n_prep + j
            rows = pl.ds(pl.multiple_of(c * ch, ch), ch)
            qc = q_ref[rows, hh * hk:(hh + 1) * hk].astype(F32)
            kc = k_ref[rows, hh * hk:(hh + 1) * hk].astype(F32)
            vc = v_ref[rows, hh * hv:(hh + 1) * hv].astype(F32)
            qn = qc * lax.rsqrt(jnp.sum(qc * qc, axis=-1, keepdims=True) + RMS_EPS) * (B_HEAD_K ** -0.5)
            kn = kc * lax.rsqrt(jnp.sum(kc * kc, axis=-1, keepdims=True) + RMS_EPS)
            knb = kn.astype(BF16)
            qk = _dot_nt(qn.astype(BF16), knb)
            kk = _dot_nt(knb, knb)
            ab = ab_ref[hh, rows, :]
            abt = abt_ref[hh, c]
            for d in range(2):
                p = dirs[d]
                a_neg, dt_bias = p["a_neg"][hh], p["dt_bias"][hh]
                beta = _sigmoid(ab[:, d:d + 1])
                g_col = a_neg * _softplus(ab[:, 2 + d:3 + d] + dt_bias)
                g_row = a_neg * _softplus(abt[2 + d:3 + d, :] + dt_bias)
                gcs = jnp.sum(jnp.where(p["incl"], jnp.broadcast_to(g_row, (ch, ch)), 0.0), axis=1, keepdims=True)
                gcs_t = jnp.sum(jnp.where(p["incl_t"], jnp.broadcast_to(g_col, (ch, ch)), 0.0), axis=0,
                                keepdims=True)
                tot = gcs[ch - 1:ch, :] if d == 0 else gcs[0:1, :]
                decay = jnp.exp(jnp.where(p["incl"], gcs - gcs_t, -jnp.inf))
                a_mat = jnp.where(p["strict"], beta * kk * decay, 0.0)
                a_diag = jnp.where(same_block, a_mat, 0.0)
                e_gcs = jnp.exp(gcs)
                chains.append(dict(
                    d=d, c=c, hh=hh, rows=rows, a_off=a_mat - a_diag, dx=-a_diag, pw=a_diag,
                    wu=jnp.concatenate([kn * (beta * e_gcs), vc * beta], axis=1),
                    kd=kn * jnp.exp(tot - gcs), qd=qn * e_gcs, qkm=(qk * decay).astype(BF16),
                    et=jnp.broadcast_to(jnp.exp(tot), (1, B_HEAD_V))))
        for _ in range(3):
            for t in chains:
                t["pw"] = mm(t["pw"], t["pw"])
            for t in chains:
                t["dx"] = compose(t["dx"], t["pw"])
        for t in chains:
            t["n"] = t["a_off"] + mm(t["dx"], t["a_off"])
        for t in chains:
            t["nn"] = mm(t["n"], t["n"])
        for t in chains:
            t["mx"] = compose(-t["n"], t["nn"])
        for t in chains:
            t["xb"] = compose(t["mx"], t["dx"]).astype(BF16)
        for t in chains:
            t["wu"] = (t["wu"] + _dot(t["xb"], t["wu"].astype(BF16))).astype(BF16)
        for t in chains:
            t["gh"] = _dot(jnp.transpose(t["kd"]).astype(BF16), t["wu"])
            t["qo"] = _dot(t["qkm"], t["wu"])
        for t in chains:
            d, c, hh = t["d"], t["c"], t["hh"]
            gq_scr[d, hh, c, 0:hk, :] = t["gh"][:, :hk].astype(BF16)
            gq_scr[d, hh, c, hk:, :] = (t["qd"] - t["qo"][:, :hk]).astype(BF16)
            h_scr[d, hh, c] = t["gh"][:, hk:]
            et_scr[d, hh, c] = t["et"]
        for fwd, bwd in zip(chains[0::2], chains[1::2]):
            hh = fwd["hh"]
            o_ref[fwd["rows"], hh * hv:(hh + 1) * hv] = fwd["qo"][:, hk:] + bwd["qo"][:, hk:]
        return carry

    lax.fori_loop(0, nc // n_prep, prepare, 0)

    for d in range(2):
        for hh in range(hps):
            if has_s0:
                s_scr[d, hh] = s0_ref[d, hh]
            else:
                s_scr[d, hh] = jnp.zeros((hk, hv), F32)

    def recur(ci, carry):
        olds = {}
        prods = {}
        for d in range(2):
            c = ci if d == 0 else nc - 1 - ci
            for hh in range(hps):
                olds[d, hh] = s_scr[d, hh]
                prods[d, hh] = _dot(gq_scr[d, hh, c], olds[d, hh].astype(BF16))
        for d in range(2):
            c = ci if d == 0 else nc - 1 - ci
            rows = pl.ds(pl.multiple_of(c * ch, ch), ch)
            for hh in range(hps):
                r = prods[d, hh]
                s_scr[d, hh] = olds[d, hh] * et_scr[d, hh, c] - r[:hk] + h_scr[d, hh, c]
                o_ref[rows, hh * hv:(hh + 1) * hv] += r[hk:]
        return carry

    lax.fori_loop(0, nc, recur, 0)
    if emit_state:
        for d in range(2):
            for hh in range(hps):
                sfin_ref[d, hh] = s_scr[d, hh]

    for hh in range(hps):
        cols = slice(hh * hv, (hh + 1) * hv)
        o = o_ref[:, cols]
        o = o * lax.rsqrt(jnp.mean(o * o, axis=-1, keepdims=True) + RMS_EPS) * ng_ref[...]
        o_ref[:, cols] = o * _silu(z_ref[:, cols].astype(F32))


def _gdn_core(proj, qkv, ab_col, ab_row, dt_bias, a_log, norm_g, s0, prev, row0, n_seq, seq_len, emit_state):
    m = proj.shape[0]
    hk, hv = B_HEAD_K, B_HEAD_V
    hps = GDN_HEADS_PER_STEP
    wk, wv = hps * hk, hps * hv
    sb = row0 // seq_len
    nc = seq_len // B_CHUNK
    in_specs = [
        pl.BlockSpec((seq_len, wk), lambda n, h: (sb + n, h)),
        pl.BlockSpec((seq_len, wk), lambda n, h: (sb + n, B_QK_DIM // wk + h)),
        pl.BlockSpec((seq_len, wv), lambda n, h: (sb + n, (2 * B_QK_DIM) // wv + h)),
        pl.BlockSpec((seq_len, wv), lambda n, h: (sb + n, B_CONV_DIM // wv + h)),
        pl.BlockSpec((hps, seq_len, 4), lambda n, h: (h, sb + n, 0)),
        pl.BlockSpec((hps, nc, 4, B_CHUNK), lambda n, h: (h, sb + n, 0, 0)),
        pl.BlockSpec(memory_space=pltpu.SMEM),
        pl.BlockSpec(memory_space=pltpu.SMEM),
        pl.BlockSpec((1, hv), lambda n, h: (0, 0)),
    ]
    args = [qkv, qkv, qkv, proj, ab_col, ab_row, dt_bias, a_log, norm_g.reshape(1, hv)]
    aliases = {}
    if s0 is not None:
        in_specs += [pl.BlockSpec((None, 2, hps, hk, hv), lambda n, h: (n, 0, h, 0, 0)),
                     pl.BlockSpec(memory_space=pl.ANY)]
        args += [s0, prev]
        aliases = {len(args) - 1: 0}
    out_specs = [pl.BlockSpec((seq_len, wv), lambda n, h: (sb + n, h))]
    out_shape = [jax.ShapeDtypeStruct((m, B_V_DIM), F32)]
    if emit_state:
        out_specs.append(pl.BlockSpec((None, 2, hps, hk, hv), lambda n, h: (n, 0, h, 0, 0)))
        out_shape.append(jax.ShapeDtypeStruct((n_seq, 2, B_N_HEADS, hk, hv), F32))
    res = pl.pallas_call(
        functools.partial(_gdn_kernel, seq_len=seq_len, has_s0=s0 is not None, emit_state=emit_state),
        grid=(n_seq, B_N_HEADS // hps),
        in_specs=in_specs,
        out_specs=out_specs,
        out_shape=out_shape,
        scratch_shapes=[
            pltpu.VMEM((2, hps, nc, hk + B_CHUNK, hk), BF16),
            pltpu.VMEM((2, hps, nc, hk, hv), F32),
            pltpu.VMEM((2, hps, nc, 1, hv), F32),
            pltpu.VMEM((2, hps, hk, hv), F32),
        ],
        input_output_aliases=aliases,
        compiler_params=_cparams(2),
        name="gdn_core",
    )(*args)
    return res if emit_state else (res[0], None)


def _gdn_mixer(y, mod, cond_of_tile, w_in, conv_w, conv_b, dt_bias, a_log, norm_g, state0, n_prompt, len_prompt,
               n_latent, len_latent):
    m = y.shape[0]
    mp = n_prompt * len_prompt
    proj = _mod_linear(y, mod, w_in, cond_of_tile)
    qkv = _conv_silu(proj, 0, B_CONV_DIM, conv_w, conv_b, mp, len_prompt, len_latent)
    ab0 = B_CONV_DIM + B_V_DIM
    ab = proj[:, ab0:ab0 + 4 * B_N_HEADS].astype(F32).reshape(m, 4, B_N_HEADS)
    ab_col = ab.transpose(2, 0, 1)
    ab_row = ab.reshape(m // B_CHUNK, B_CHUNK, 4, B_N_HEADS).transpose(3, 0, 2, 1)
    dt_bias = dt_bias.astype(F32)
    a_log = a_log.astype(F32)
    out, st = _gdn_core(proj, qkv, ab_col, ab_row, dt_bias, a_log, norm_g, None, None, 0, n_prompt, len_prompt, True)
    out, _ = _gdn_core(proj, qkv, ab_col, ab_row, dt_bias, a_log, norm_g, state0, out, mp, n_latent, len_latent,
                       False)
    return out, st


KV_PAIR = 2
PAIR_Q = KV_PAIR * C_GROUP * C_HEAD_DIM
PAIR_KV = KV_PAIR * C_HEAD_DIM


def _softmax_pv(heads, values):
    mxs = []
    for scores, sink in heads:
        mx = sink
        for s in scores:
            mx = jnp.maximum(mx, jnp.max(s, axis=-1, keepdims=True))
        mxs.append(mx)
    probs = [[jnp.exp(s - mx) for s in scores] for (scores, _), mx in zip(heads, mxs)]
    dens = []
    for (_, sink), mx, ps in zip(heads, mxs, probs):
        den = jnp.exp(sink - mx)
        for p in ps:
            den = den + jnp.sum(p, axis=-1, keepdims=True)
        dens.append(den)
    outs = []
    for ps in probs:
        acc = None
        for p, v in zip(ps, values):
            pv = _dot(p.astype(BF16), v)
            acc = pv if acc is None else acc + pv
        outs.append(acc)
    return [acc / den for acc, den in zip(outs, dens)]


def _attn_ctx_kernel(q_ref, k_ref, v_ref, sink_ref, o_ref):
    pair = pl.program_id(1)
    hd = C_HEAD_DIM
    for gg in range(KV_PAIR):
        kb = k_ref[:, gg * hd:(gg + 1) * hd].astype(BF16)
        vb = v_ref[:, gg * hd:(gg + 1) * hd].astype(BF16)
        heads = []
        for r in range(C_GROUP):
            col = (gg * C_GROUP + r) * hd
            qb = q_ref[:, col:col + hd].astype(BF16)
            sink = sink_ref[pair * (KV_PAIR * C_GROUP) + gg * C_GROUP + r]
            heads.append(([_dot_nt(qb, kb) * C_SCALE], sink))
        for r, o in enumerate(_softmax_pv(heads, [vb])):
            col = (gg * C_GROUP + r) * hd
            o_ref[:, col:col + hd] = o


def _attn_ctx(proj, sink, n_prompt, len_prompt):
    m = proj.shape[0]
    k_block0 = C_Q_DIM // PAIR_KV
    v_block0 = (C_Q_DIM + C_KV_DIM) // PAIR_KV
    return pl.pallas_call(
        _attn_ctx_kernel,
        grid=(n_prompt, C_N_KV // KV_PAIR),
        in_specs=[
            pl.BlockSpec((len_prompt, PAIR_Q), lambda n, p: (n, p)),
            pl.BlockSpec((len_prompt, PAIR_KV), lambda n, p: (n, k_block0 + p)),
            pl.BlockSpec((len_prompt, PAIR_KV), lambda n, p: (n, v_block0 + p)),
            pl.BlockSpec(memory_space=pltpu.SMEM),
        ],
        out_specs=pl.BlockSpec((len_prompt, PAIR_Q), lambda n, p: (n, p)),
        out_shape=jax.ShapeDtypeStruct((m, C_Q_DIM), F32),
        compiler_params=_cparams(2),
        name="attn_ctx",
    )(proj, proj, proj, sink)


def _rope(x, cos, sin):
    width = x.shape[1]
    lane = lax.broadcasted_iota(jnp.int32, x.shape, 1)
    nf = C_HEAD_DIM // 4
    partner = jnp.where(lane % (2 * nf) < nf, pltpu.roll(x, width - nf, 1), pltpu.roll(x, nf, 1))
    return x * cos + partner * sin


def _attn_lat_kernel(q_ref, kp_ref, vp_ref, kc_ref, vc_ref, cq_ref, sq_ref, ck_ref, sk_ref, sink_ref, prev_ref, o_ref,
                     *, seq_len):
    del prev_ref
    pair = pl.program_id(1)
    qi = pl.program_id(2)
    hd = C_HEAD_DIM
    blk = C_BLOCK
    span = C_BLOCK + 2 * C_WINDOW
    start = pl.multiple_of(qi * blk, blk)
    q = _rope(q_ref[...].astype(F32), cq_ref[...], sq_ref[...])
    kw = _rope(kp_ref[pl.ds(start, span), :].astype(F32), ck_ref[pl.ds(start, span), :],
               sk_ref[pl.ds(start, span), :])
    vw = vp_ref[pl.ds(start, span), :]
    iq = lax.broadcasted_iota(jnp.int32, (blk, span), 0)
    jk = lax.broadcasted_iota(jnp.int32, (blk, span), 1)
    kpos = start - C_WINDOW + jk
    ok = jnp.logical_and(jnp.abs(iq + C_WINDOW - jk) <= C_WINDOW, jnp.logical_and(kpos >= 0, kpos < seq_len))
    for gg in range(KV_PAIR):
        kb = kw[:, gg * hd:(gg + 1) * hd].astype(BF16)
        vb = vw[:, gg * hd:(gg + 1) * hd].astype(BF16)
        kcb = kc_ref[:, gg * hd:(gg + 1) * hd].astype(BF16)
        vcb = vc_ref[:, gg * hd:(gg + 1) * hd].astype(BF16)
        heads = []
        for r in range(C_GROUP):
            col = (gg * C_GROUP + r) * hd
            qb = q[:, col:col + hd].astype(BF16)
            s_loc = jnp.where(ok, _dot_nt(qb, kb) * C_SCALE, -jnp.inf)
            s_ctx = _dot_nt(qb, kcb) * C_SCALE
            sink = sink_ref[pair * (KV_PAIR * C_GROUP) + gg * C_GROUP + r]
            heads.append(([s_loc, s_ctx], sink))
        for r, o in enumerate(_softmax_pv(heads, [vb, vcb])):
            col = (gg * C_GROUP + r) * hd
            o_ref[:, col:col + hd] = o


def _rope_tables(seq_len):
    nf = C_HEAD_DIM // 4
    t = jnp.arange(seq_len)
    inv_freq = ROPE_BASE ** (-jnp.arange(nf, dtype=F32) / nf)
    ang_r = (t // GRID_W).astype(F32)[:, None] * inv_freq
    ang_c = (t % GRID_W).astype(F32)[:, None] * inv_freq
    cos = jnp.concatenate([jnp.cos(ang_r), jnp.cos(ang_r), jnp.cos(ang_c), jnp.cos(ang_c)], axis=1)
    sin = jnp.concatenate([-jnp.sin(ang_r), jnp.sin(ang_r), -jnp.sin(ang_c), jnp.sin(ang_c)], axis=1)
    return cos, sin


def _attn_lat(proj, cache_k, cache_v, sink, prev, row0, n_latent, len_latent):
    m = proj.shape[0]
    past = cache_k.shape[1]
    w = C_WINDOW
    k_lat = proj[row0:, C_Q_DIM:C_Q_DIM + C_KV_DIM].reshape(n_latent, len_latent, C_KV_DIM)
    v_lat = proj[row0:, C_Q_DIM + C_KV_DIM:C_Q_DIM + 2 * C_KV_DIM].reshape(n_latent, len_latent, C_KV_DIM)
    kp = jnp.pad(k_lat, ((0, 0), (w, w), (0, 0)))
    vp = jnp.pad(v_lat, ((0, 0), (w, w), (0, 0)))
    kc = cache_k.reshape(n_latent, past, C_KV_DIM).astype(F32)
    vc = cache_v.reshape(n_latent, past, C_KV_DIM).astype(F32)
    cos, sin = _rope_tables(len_latent)
    cos_q = jnp.tile(cos, (1, PAIR_Q // C_HEAD_DIM))
    sin_q = jnp.tile(sin, (1, PAIR_Q // C_HEAD_DIM))
    cos_k = jnp.pad(jnp.tile(cos, (1, KV_PAIR)), ((w, w), (0, 0)))
    sin_k = jnp.pad(jnp.tile(sin, (1, KV_PAIR)), ((w, w), (0, 0)))
    qb0 = row0 // C_BLOCK
    nqb = len_latent // C_BLOCK
    padded = len_latent + 2 * w
    return pl.pallas_call(
        functools.partial(_attn_lat_kernel, seq_len=len_latent),
        grid=(n_latent, C_N_KV // KV_PAIR, nqb),
        in_specs=[
            pl.BlockSpec((C_BLOCK, PAIR_Q), lambda n, p, i: (qb0 + n * nqb + i, p)),
            pl.BlockSpec((None, padded, PAIR_KV), lambda n, p, i: (n, 0, p)),
            pl.BlockSpec((None, padded, PAIR_KV), lambda n, p, i: (n, 0, p)),
            pl.BlockSpec((None, past, PAIR_KV), lambda n, p, i: (n, 0, p)),
            pl.BlockSpec((None, past, PAIR_KV), lambda n, p, i: (n, 0, p)),
            pl.BlockSpec((C_BLOCK, PAIR_Q), lambda n, p, i: (i, 0)),
            pl.BlockSpec((C_BLOCK, PAIR_Q), lambda n, p, i: (i, 0)),
            pl.BlockSpec((padded, PAIR_KV), lambda n, p, i: (0, 0)),
            pl.BlockSpec((padded, PAIR_KV), lambda n, p, i: (0, 0)),
            pl.BlockSpec(memory_space=pltpu.SMEM),
            pl.BlockSpec(memory_space=pl.ANY),
        ],
        out_specs=pl.BlockSpec((C_BLOCK, PAIR_Q), lambda n, p, i: (qb0 + n * nqb + i, p)),
        out_shape=jax.ShapeDtypeStruct((m, C_Q_DIM), F32),
        input_output_aliases={10: 0},
        compiler_params=_cparams(3),
        name="attn_lat",
    )(proj, kp, vp, kc, vc, cos_q, sin_q, cos_k, sin_k, sink, prev)


def _attn_mixer(y, mod, cond_of_tile, w_in, sink, cache_k, cache_v, n_prompt, len_prompt, n_latent, len_latent):
    m = y.shape[0]
    mp = n_prompt * len_prompt
    proj = _mod_linear(y, mod, w_in, cond_of_tile)
    sink = sink.astype(F32)
    out = _attn_ctx(proj, sink, n_prompt, len_prompt)
    out = _attn_lat(proj, cache_k, cache_v, sink, out, mp, n_latent, len_latent)
    new_k = proj[:mp, C_Q_DIM:C_Q_DIM + C_KV_DIM].astype(F32).reshape(n_prompt, len_prompt, C_N_KV, C_HEAD_DIM)
    new_v = proj[:mp, C_Q_DIM + C_KV_DIM:C_Q_DIM + 2 * C_KV_DIM].astype(F32)
    new_v = new_v.reshape(n_prompt, len_prompt, C_N_KV, C_HEAD_DIM)
    return out, new_k, new_v


def kernel(x_prompt, x_sample, state_ssd, state_delta, cache_k, cache_v, c, c_ctx, w_mod, b_mod, ln_g, ln_b, ffn_w_gate, ffn_w_up, ffn_w_down, ssd_w_in, ssd_conv_w, ssd_conv_b, ssd_dt_bias, ssd_a_log, ssd_d, ssd_norm, ssd_w_out, gdn_w_in, gdn_conv_w, gdn_conv_b, gdn_dt_bias, gdn_a_log, gdn_norm, gdn_w_out, attn_w_in, attn_sink, attn_w_out):
    n_prompt, len_prompt, d = x_prompt.shape
    n_latent, len_latent, _ = x_sample.shape
    mp = n_prompt * len_prompt
    assert d == D_MODEL and n_latent + 1 <= N_COND
    assert len_prompt % ROW_TILE == 0 or ROW_TILE % len_prompt == 0
    assert mp % ROW_TILE == 0 and len_latent % ROW_TILE == 0
    assert mp % len_latent == 0 and len_latent % len_prompt == 0 and len_latent % GRID_W == 0

    n_prompt_tiles = mp // ROW_TILE
    tiles_per_request = len_latent // ROW_TILE
    cond_of_tile = functools.partial(_cond_index, n_prompt_tiles=n_prompt_tiles, tiles_per_request=tiles_per_request)

    cond = jnp.concatenate([c_ctx[None].astype(F32), c.astype(F32),
                            jnp.zeros((N_COND - 1 - n_latent, d), F32)], axis=0)
    mods = _adaln(cond, w_mod, b_mod).reshape(DEPTH, N_COND, N_MOD, d)

    y = jnp.concatenate([x_prompt.reshape(mp, d), x_sample.reshape(n_latent * len_latent, d)], axis=0)
    seqs = (n_prompt, len_prompt, n_latent, len_latent)
    ssd_states, gdn_states, k_list, v_list = [], [], [], []
    for i in range(DEPTH):
        mod = mods[i]
        y = _ffn_half(y, mod, 0, ln_g[i, 0], ln_b[i, 0], ffn_w_gate[i, 0], ffn_w_up[i, 0], ffn_w_down[i, 0],
                      cond_of_tile)
        kind, j = i % 3, i // 3
        if kind == 0:
            a, st = _ssd_mixer(y, mod, cond_of_tile, ssd_w_in[j], ssd_conv_w[j], ssd_conv_b[j], ssd_dt_bias[j],
                               ssd_a_log[j], ssd_d[j], state_ssd[:, j], *seqs)
            ssd_states.append(st)
            y = _out_proj_ln(a, y, mod, ssd_w_out[j], ln_g[i, 1], ln_b[i, 1], cond_of_tile, rms_g=ssd_norm[j])
        elif kind == 1:
            a, st = _gdn_mixer(y, mod, cond_of_tile, gdn_w_in[j], gdn_conv_w[j], gdn_conv_b[j], gdn_dt_bias[j],
                               gdn_a_log[j], gdn_norm[j], state_delta[:, j], *seqs)
            gdn_states.append(st)
            y = _out_proj_ln(a, y, mod, gdn_w_out[j], ln_g[i, 1], ln_b[i, 1], cond_of_tile)
        else:
            a, kc, vc = _attn_mixer(y, mod, cond_of_tile, attn_w_in[j], attn_sink[j], cache_k[:, j], cache_v[:, j],
                                    *seqs)
            k_list.append(kc)
            v_list.append(vc)
            y = _out_proj_ln(a, y, mod, attn_w_out[j], ln_g[i, 1], ln_b[i, 1], cond_of_tile)
        y = _ffn_half(y, mod, 2, ln_g[i, 2], ln_b[i, 2], ffn_w_gate[i, 1], ffn_w_up[i, 1], ffn_w_down[i, 1],
                      cond_of_tile)

    y_prompt = y[:mp].reshape(n_prompt, len_prompt, d)
    y_sample = y[mp:].reshape(n_latent, len_latent, d)
    return (y_prompt, y_sample, jnp.stack(ssd_states, axis=1), jnp.stack(gdn_states, axis=1),
            jnp.stack(k_list, axis=1), jnp.stack(v_list, axis=1))
```

```python
import functools
import math

import jax
import jax.numpy as jnp
from jax import lax
from jax.experimental import pallas as pl
from jax.experimental.pallas import tpu as pltpu

F32 = jnp.float32
BF16 = jnp.bfloat16

D_MODEL = 1024
DEPTH = 4
GRID_W = 64
N_MOD = 9
D_FF = 2816
DEEPNORM_ALPHA = (2 * DEPTH) ** 0.25
LN_EPS = 1e-5
RMS_EPS = 1e-6
FFN_RES = 0.5

A_D_INNER = 2 * D_MODEL
A_HEAD_DIM = 64
A_N_HEADS = A_D_INNER // A_HEAD_DIM
A_N_GROUPS = 4
A_HEADS_PER_GROUP = A_N_HEADS // A_N_GROUPS
A_GROUP_DIM = A_HEADS_PER_GROUP * A_HEAD_DIM
A_D_STATE = 128
A_CHUNK = 128
A_CONV_DIM = A_D_INNER + 2 * A_N_GROUPS * A_D_STATE

B_N_HEADS = 8
B_HEAD_K = 128
B_HEAD_V = 256
B_CHUNK = 64
B_QK_DIM = B_N_HEADS * B_HEAD_K
B_V_DIM = B_N_HEADS * B_HEAD_V
B_CONV_DIM = 2 * B_QK_DIM + B_V_DIM
GDN_SOLVE_BLOCK = 16
GDN_PREP_UNROLL = 4
GDN_HEADS_PER_STEP = 2

C_N_HEADS = 16
C_N_KV = 4
C_GROUP = C_N_HEADS // C_N_KV
C_HEAD_DIM = 64
C_WINDOW = 128
C_BLOCK = 128
C_Q_DIM = C_N_HEADS * C_HEAD_DIM
C_KV_DIM = C_N_KV * C_HEAD_DIM
C_SCALE = C_HEAD_DIM ** -0.5
ROPE_BASE = 10000.0

N_COND = 16
ROW_TILE = 512
PROJ_TILE_MAX = 3328
PROJ_ALIGN = 256
FF_TILE = 1408
CONV_KERNEL = 5
CONV_PAD = 16
CONV_COLS = 256
VMEM_LIMIT = 56 * 1024 * 1024


def _cparams(n_axes):
    return pltpu.CompilerParams(dimension_semantics=("arbitrary",) * n_axes, vmem_limit_bytes=VMEM_LIMIT)


def _dot(a, b, precision=None):
    return lax.dot_general(a, b, (((1,), (0,)), ((), ())), precision=precision, preferred_element_type=F32)


def _dot_nt(a, b, precision=None):
    return lax.dot_general(a, b, (((1,), (1,)), ((), ())), precision=precision, preferred_element_type=F32)


def _dot_tn(a, b, precision=None):
    return lax.dot_general(a, b, (((0,), (0,)), ((), ())), precision=precision, preferred_element_type=F32)


def _split3(x):
    hi = x.astype(BF16)
    rest = x - hi.astype(F32)
    mid = rest.astype(BF16)
    lo = (rest - mid.astype(F32)).astype(BF16)
    return [hi, mid, lo]


def _sigmoid(x):
    return 1.0 / (1.0 + jnp.exp(-x))


def _silu(x):
    return x * _sigmoid(x)


def _softplus(x):
    return jnp.maximum(x, 0.0) + jnp.log(1.0 + jnp.exp(-jnp.abs(x)))


def _layer_norm(t, g, b):
    mu = jnp.mean(t, axis=-1, keepdims=True)
    tc = t - mu
    var = jnp.mean(tc * tc, axis=-1, keepdims=True)
    return tc * lax.rsqrt(var + LN_EPS) * g + b


def _cond_index(i, n_prompt_tiles, tiles_per_request):
    return jnp.where(i < n_prompt_tiles, 0, 1 + (jnp.maximum(i - n_prompt_tiles, 0)) // tiles_per_request)


def _adaln_kernel(c_ref, w_ref, b_ref, o_ref):
    c = c_ref[...]
    h = _silu(c).astype(BF16)
    o_ref[...] = _dot(h, w_ref[...].astype(BF16)) + b_ref[...]


def _adaln(cond, w_mod, b_mod):
    n_out = N_MOD * D_MODEL
    tn = D_MODEL
    return pl.pallas_call(
        _adaln_kernel,
        grid=(DEPTH, n_out // tn),
        in_specs=[
            pl.BlockSpec((N_COND, D_MODEL), lambda l, j: (0, 0)),
            pl.BlockSpec((None, D_MODEL, tn), lambda l, j: (l, 0, j)),
            pl.BlockSpec((None, 1, tn), lambda l, j: (l, 0, j)),
        ],
        out_specs=pl.BlockSpec((None, N_COND, tn), lambda l, j: (l, 0, j)),
        out_shape=jax.ShapeDtypeStruct((DEPTH, N_COND, n_out), F32),
        compiler_params=_cparams(2),
        name="adaln",
    )(cond, w_mod, b_mod.reshape(DEPTH, 1, n_out))


def _ffn_kernel(*refs, s, n_in, n_out, n_prompt_tiles):
    y_refs = refs[:n_in]
    mod_ref, g_ref, b_ref, wg_ref, wu_ref, wd_ref = refs[n_in:n_in + 6]
    o_refs = refs[n_in + 6:]
    is_prompt = pl.program_id(0) < n_prompt_tiles
    if n_in == 1:
        y = y_refs[0][...]
    else:
        y = jnp.where(is_prompt, y_refs[0][...], y_refs[1][...])
    shift = mod_ref[pl.ds(3 * s, 1), :]
    scale = mod_ref[pl.ds(3 * s + 1, 1), :]
    gate = mod_ref[pl.ds(3 * s + 2, 1), :]
    h = (y * (1.0 + scale) + shift).astype(BF16)
    acc = None
    for c in range(D_FF // FF_TILE):
        cols = slice(c * FF_TILE, (c + 1) * FF_TILE)
        a = _dot(h, wg_ref[:, cols])
        u = _dot(h, wu_ref[:, cols])
        f = (_silu(a) * u).astype(BF16)
        part = _dot(f, wd_ref[cols, :])
        acc = part if acc is None else acc + part
    t = DEEPNORM_ALPHA * y + (FFN_RES * gate) * acc
    res = _layer_norm(t, g_ref[...], b_ref[...])
    if n_out == 1:
        o_refs[0][...] = res
    else:
        @pl.when(is_prompt)
        def _():
            o_refs[0][...] = res

        @pl.when(jnp.logical_not(is_prompt))
        def _():
            o_refs[1][...] = res


def _ffn_half(ys, mod, s, g, b, w_gate, w_up, w_down, layer, half, cond_of_tile, n_prompt_rows, split_out=False):
    npt = n_prompt_rows // ROW_TILE
    split_in = isinstance(ys, (tuple, list))
    ys = tuple(ys) if split_in else (ys,)
    m = sum(a.shape[0] for a in ys)
    resident = pl.Buffered(1)

    def whole(i):
        return (i, 0)

    def prompt_part(i):
        return (jnp.minimum(i, npt - 1), 0)

    def latent_part(i):
        return (jnp.maximum(i - npt, 0), 0)

    def row_specs(split):
        return [pl.BlockSpec((ROW_TILE, D_MODEL), f) for f in ((prompt_part, latent_part) if split else (whole,))]

    if split_out:
        out_shape = [jax.ShapeDtypeStruct((n_prompt_rows, D_MODEL), F32),
                     jax.ShapeDtypeStruct((m - n_prompt_rows, D_MODEL), F32)]
    else:
        out_shape = [jax.ShapeDtypeStruct((m, D_MODEL), F32)]
    res = pl.pallas_call(
        functools.partial(_ffn_kernel, s=s, n_in=len(ys), n_out=len(out_shape), n_prompt_tiles=npt),
        grid=(m // ROW_TILE,),
        in_specs=row_specs(split_in) + [
            pl.BlockSpec((None, N_MOD, D_MODEL), lambda i: (cond_of_tile(i), 0, 0)),
            pl.BlockSpec((1, D_MODEL), lambda i: (0, 0)),
            pl.BlockSpec((1, D_MODEL), lambda i: (0, 0)),
            pl.BlockSpec((None, None, D_MODEL, D_FF), lambda i: (layer, half, 0, 0), pipeline_mode=resident),
            pl.BlockSpec((None, None, D_MODEL, D_FF), lambda i: (layer, half, 0, 0), pipeline_mode=resident),
            pl.BlockSpec((None, None, D_FF, D_MODEL), lambda i: (layer, half, 0, 0), pipeline_mode=resident),
        ],
        out_specs=row_specs(split_out),
        out_shape=out_shape,
        compiler_params=_cparams(1),
        name="ffn_half",
    )(*ys, mod, g.reshape(1, -1), b.reshape(1, -1), w_gate, w_up, w_down)
    return tuple(res) if split_out else res[0]


def _modlin_kernel(y_ref, mod_ref, w_ref, o_ref):
    shift = mod_ref[pl.ds(3, 1), :]
    scale = mod_ref[pl.ds(4, 1), :]
    h = (y_ref[...] * (1.0 + scale) + shift).astype(BF16)
    o_ref[...] = _dot(h, w_ref[...]).astype(o_ref.dtype)


def _mod_linear(y, mod, w, cond_of_tile):
    m = y.shape[0]
    n = w.shape[1]
    n_tiles = -(-n // PROJ_TILE_MAX)
    tn = -(-n // (n_tiles * PROJ_ALIGN)) * PROJ_ALIGN
    n_pad = n_tiles * tn
    w = jnp.pad(w.astype(BF16), ((0, 0), (0, n_pad - n)))
    return pl.pallas_call(
        _modlin_kernel,
        grid=(n_tiles, m // ROW_TILE),
        in_specs=[
            pl.BlockSpec((ROW_TILE, D_MODEL), lambda j, i: (i, 0)),
            pl.BlockSpec((None, N_MOD, D_MODEL), lambda j, i: (cond_of_tile(i), 0, 0)),
            pl.BlockSpec((D_MODEL, tn), lambda j, i: (0, j)),
        ],
        out_specs=pl.BlockSpec((ROW_TILE, tn), lambda j, i: (i, j)),
        out_shape=jax.ShapeDtypeStruct((m, n_pad), BF16),
        compiler_params=_cparams(2),
        name="mod_linear",
    )(y, mod, w)


def _outproj_kernel(a_ref, y_ref, mod_ref, w_ref, g_ref, b_ref, *rest, rms):
    if rms:
        ng_ref, o_ref = rest
    else:
        (o_ref,) = rest
    a = a_ref[...]
    if rms:
        a = a * lax.rsqrt(jnp.mean(a * a, axis=-1, keepdims=True) + RMS_EPS) * ng_ref[...]
    mix = _dot(a.astype(BF16), w_ref[...])
    gate = mod_ref[pl.ds(5, 1), :]
    t = DEEPNORM_ALPHA * y_ref[...] + gate * mix
    o_ref[...] = _layer_norm(t, g_ref[...], b_ref[...])


def _out_proj_ln(a, y, mod, w, g, b, cond_of_tile, rms_g=None):
    m, k = a.shape
    in_specs = [
        pl.BlockSpec((ROW_TILE, k), lambda i: (i, 0)),
        pl.BlockSpec((ROW_TILE, D_MODEL), lambda i: (i, 0)),
        pl.BlockSpec((None, N_MOD, D_MODEL), lambda i: (cond_of_tile(i), 0, 0)),
        pl.BlockSpec((k, D_MODEL), lambda i: (0, 0)),
        pl.BlockSpec((1, D_MODEL), lambda i: (0, 0)),
        pl.BlockSpec((1, D_MODEL), lambda i: (0, 0)),
    ]
    args = [a, y, mod, w.astype(BF16), g.reshape(1, -1), b.reshape(1, -1)]
    if rms_g is not None:
        in_specs.append(pl.BlockSpec((1, k), lambda i: (0, 0)))
        args.append(rms_g.reshape(1, -1))
    return pl.pallas_call(
        functools.partial(_outproj_kernel, rms=rms_g is not None),
        grid=(m // ROW_TILE,),
        in_specs=in_specs,
        out_specs=pl.BlockSpec((ROW_TILE, D_MODEL), lambda i: (i, 0)),
        out_shape=jax.ShapeDtypeStruct((m, D_MODEL), F32),
        compiler_params=_cparams(1),
        name="out_proj_ln",
    )(*args)


def _conv_kernel(x_ref, w_ref, b_ref, o_ref, xs_scr, *, n_prompt_tiles, len_prompt):
    rows = x_ref.shape[0]
    pad = CONV_PAD
    half = CONV_KERNEL // 2
    zeros = jnp.zeros((pad, xs_scr.shape[1]), F32)
    xs_scr[0:pad, :] = zeros
    xs_scr[pad + rows:, :] = zeros
    xs_scr[pad:pad + rows, :] = x_ref[...].astype(F32)

    def taps(start, n, valid=None):
        acc = b_ref[...]
        for k in range(CONV_KERNEL):
            xk = xs_scr[pl.ds(pad + start + k - half, n), :]
            if valid is not None:
                xk = jnp.where(valid[k], xk, 0.0)
            acc = acc + w_ref[pl.ds(k, 1), :] * xk
        return _silu(acc).astype(o_ref.dtype)

    o_ref[...] = taps(0, rows)

    @pl.when(pl.program_id(0) < n_prompt_tiles)
    def _():
        i = lax.broadcasted_iota(jnp.int32, (2 * pad, 1), 0)
        valid = [(i < pad) == (i + k - half < pad) for k in range(CONV_KERNEL)]
        for b in range(1, rows // len_prompt):
            r = b * len_prompt
            o_ref[r - pad:r + pad, :] = taps(r - pad, 2 * pad, valid)


def _conv_silu(proj, col0, width, conv_w, conv_b, n_prompt_rows, len_prompt, len_latent):
    m = proj.shape[0]
    rows = len_latent
    c0 = col0 // CONV_COLS
    return pl.pallas_call(
        functools.partial(_conv_kernel, n_prompt_tiles=n_prompt_rows // rows, len_prompt=len_prompt),
        grid=(m // rows, width // CONV_COLS),
        in_specs=[
            pl.BlockSpec((rows, CONV_COLS), lambda i, j: (i, c0 + j)),
            pl.BlockSpec((CONV_KERNEL, CONV_COLS), lambda i, j: (0, j)),
            pl.BlockSpec((1, CONV_COLS), lambda i, j: (0, j)),
        ],
        out_specs=pl.BlockSpec((rows, CONV_COLS), lambda i, j: (i, j)),
        out_shape=jax.ShapeDtypeStruct((m, width), BF16),
        scratch_shapes=[pltpu.VMEM((rows + 2 * CONV_PAD, CONV_COLS), F32)],
        compiler_params=_cparams(2),
        name="conv_silu",
    )(proj, conv_w, conv_b.reshape(1, -1))


def _ssd_kernel(*refs, seq_len, has_h0, emit_state, n_unread):
    x_ref, z_ref, b_ref, c_ref, dt_ref, dtt_ref, bias_ref, biast_ref, alog_ref, alogt_ref, dskip_ref = refs[:11]
    pos = 11
    h0_ref = None
    if has_h0:
        h0_ref = refs[pos]
        pos += 1
    pos += n_unread
    y_ref = refs[pos]
    pos += 1
    hfin_ref = None
    if emit_state:
        hfin_ref = refs[pos]
        pos += 1
    s_scr = refs[pos]

    q = A_CHUNK
    nc = seq_len // q
    hpg = A_HEADS_PER_GROUP
    hd = A_HEAD_DIM
    ii = lax.broadcasted_iota(jnp.int32, (q, q), 0)
    jj = lax.broadcasted_iota(jnp.int32, (q, q), 1)
    lower = ii >= jj
    upper = ii <= jj
    t_lower = jnp.where(lower, 1.0, 0.0).astype(BF16)
    t_upper = jnp.where(upper, 1.0, 0.0).astype(BF16)
    t_lower_k = jnp.concatenate([t_lower] * 3, axis=1)
    t_upper_k = jnp.concatenate([t_upper] * 3, axis=1)
    t_lower_r = jnp.concatenate([t_lower] * 3, axis=0)
    t_upper_r = jnp.concatenate([t_upper] * 3, axis=0)
    er = lax.broadcasted_iota(jnp.int32, (3 * hpg, hpg * hd), 0)
    ec = lax.broadcasted_iota(jnp.int32, (3 * hpg, hpg * hd), 1)
    expand = jnp.where(ec // hd == er % hpg, 1.0, 0.0).astype(BF16)

    first_head = lax.broadcasted_iota(jnp.int32, (q, 2 * hd), 1) < hd

    def widen(v):
        return _dot(jnp.concatenate(_split3(v), axis=1), expand)

    y_ref[...] = dskip_ref[...] * x_ref[...].astype(F32)

    par = []
    for d in range(2):
        par.append(dict(
            bias=bias_ref[:, d * hpg:(d + 1) * hpg],
            a_neg=-jnp.exp(alog_ref[:, d * hpg:(d + 1) * hpg]),
            bias_t=biast_ref[d * hpg:(d + 1) * hpg, :],
            a_neg_t=-jnp.exp(alogt_ref[d * hpg:(d + 1) * hpg, :])))
        if has_h0:
            s_scr[d] = jnp.transpose(h0_ref[d].reshape(hpg * hd, A_D_STATE))
        else:
            s_scr[d] = jnp.zeros((A_D_STATE, hpg * hd), F32)

    def body(ci, carry):
        ts = []
        for d in range(2):
            p = par[d]
            c = ci if d == 0 else nc - 1 - ci
            r0 = pl.multiple_of(c * q, q)
            dtc = _softplus(dt_ref[pl.ds(r0, q), d * hpg:(d + 1) * hpg] + p["bias"])
            dta = jnp.concatenate(_split3(dtc * p["a_neg"]), axis=0)
            dtc_t = _softplus(dtt_ref[c][d * hpg:(d + 1) * hpg, :] + p["bias_t"])
            dta_t = jnp.concatenate(_split3(dtc_t * p["a_neg_t"]), axis=1)
            ts.append(dict(d=d, rows=pl.ds(r0, q), dtc=dtc, dta=dta, dta_t=dta_t, mask=lower if d == 0 else upper))
        for t in ts:
            if t["d"] == 0:
                t["acs"] = _dot(t_lower_k, t["dta"])
                t["acs_t"] = _dot(t["dta_t"], t_upper_r)
            else:
                t["acs"] = _dot(t_upper_k, t["dta"])
                t["acs_t"] = _dot(t["dta_t"], t_lower_r)
            t["dt_w"] = widen(t["dtc"])
        for t in ts:
            t["bb"] = b_ref[t["rows"], :].astype(BF16)
            t["cc"] = c_ref[t["rows"], :].astype(BF16)
            t["cb"] = _dot_nt(t["cc"], t["bb"])
            t["s_old"] = s_scr[t["d"]]
            t["cs"] = _dot(t["cc"], t["s_old"].astype(BF16))
            t["xdt"] = x_ref[t["rows"], :].astype(F32) * t["dt_w"]
            t["xdt_b"] = t["xdt"].astype(BF16)
        for t in ts:
            t["acs_w"] = widen(t["acs"])
            t["tot_w"] = t["acs_w"][q - 1:q, :] if t["d"] == 0 else t["acs_w"][0:1, :]
        for t in ts:
            t["y_diag"] = []
        for pr in range(hpg // 2):
            for t in ts:
                mms = []
                for r in (2 * pr, 2 * pr + 1):
                    seg = t["acs"][:, r:r + 1] - t["acs_t"][r:r + 1, :]
                    lmat = jnp.exp(jnp.where(t["mask"], seg, -jnp.inf))
                    mms.append((t["cb"] * lmat).astype(BF16))
                x_pair = t["xdt_b"][:, 2 * pr * hd:2 * (pr + 1) * hd]
                x_diag = jnp.concatenate([jnp.where(first_head, x_pair, jnp.zeros_like(x_pair)),
                                          jnp.where(first_head, jnp.zeros_like(x_pair), x_pair)], axis=0)
                t["y_diag"].append(_dot(jnp.concatenate(mms, axis=1), x_diag))
        for t in ts:
            y_off = jnp.exp(t["acs_w"]) * t["cs"]
            y_ref[t["rows"], :] += jnp.concatenate(t["y_diag"], axis=1) + y_off
            xs = (t["xdt"] * jnp.exp(t["tot_w"] - t["acs_w"])).astype(BF16)
            s_scr[t["d"]] = t["s_old"] * jnp.exp(t["tot_w"]) + _dot_tn(t["bb"], xs)
        return carry

    lax.fori_loop(0, nc, body, 0)
    if emit_state:
        for d in range(2):
            hfin_ref[d] = jnp.transpose(s_scr[d]).reshape(hpg, hd, A_D_STATE)

    y_ref[...] = y_ref[...] * _silu(z_ref[...].astype(F32))


def _ssd_core(proj, xbc, dt_col, dt_row, params, layer, n_layers, h0, prev, states_prev, row0, n_seq, seq_len,
              emit_state):
    m = proj.shape[0]
    gd = A_GROUP_DIM
    ns = A_D_STATE
    hpg = A_HEADS_PER_GROUP
    sb = row0 // seq_len
    cb0 = row0 // A_CHUNK
    nc = seq_len // A_CHUNK
    x_blocks = A_D_INNER // gd
    bias_c, bias_r, alog_c, alog_r, dskip = params
    in_specs = [
        pl.BlockSpec((seq_len, gd), lambda n, g: (sb + n, g)),
        pl.BlockSpec((seq_len, gd), lambda n, g: (sb + n, g)),
        pl.BlockSpec((seq_len, ns), lambda n, g: (sb + n, x_blocks * (gd // ns) + g)),
        pl.BlockSpec((seq_len, ns), lambda n, g: (sb + n, x_blocks * (gd // ns) + A_N_GROUPS + g)),
        pl.BlockSpec((None, seq_len, 2 * hpg), lambda n, g: (g, sb + n, 0)),
        pl.BlockSpec((None, nc, 2 * hpg, A_CHUNK), lambda n, g: (g, sb + n, 0, 0)),
        pl.BlockSpec((None, 1, 2 * hpg), lambda n, g: (g, 0, 0)),
        pl.BlockSpec((None, 2 * hpg, 1), lambda n, g: (g, 0, 0)),
        pl.BlockSpec((None, 1, 2 * hpg), lambda n, g: (g, 0, 0)),
        pl.BlockSpec((None, 2 * hpg, 1), lambda n, g: (g, 0, 0)),
        pl.BlockSpec((None, 1, gd), lambda n, g: (g, 0, 0)),
    ]
    args = [xbc, proj, xbc, xbc, dt_col, dt_row, bias_c, bias_r, alog_c, alog_r, dskip]
    aliases = {}
    state_spec = pl.BlockSpec((None, None, 2, hpg, A_HEAD_DIM, ns), lambda n, g: (n, layer, 0, g, 0, 0))
    if h0 is not None:
        in_specs.append(state_spec)
        args.append(h0)
    n_unread = 0
    for arr, out_idx in ((prev, 0), (states_prev, 1)):
        if arr is not None:
            in_specs.append(pl.BlockSpec(memory_space=pl.ANY))
            args.append(arr)
            aliases[len(args) - 1] = out_idx
            n_unread += 1
    out_specs = [pl.BlockSpec((seq_len, gd), lambda n, g: (sb + n, g))]
    out_shape = [jax.ShapeDtypeStruct((m, A_D_INNER), F32)]
    if emit_state:
        out_specs.append(state_spec)
        out_shape.append(jax.ShapeDtypeStruct((n_seq, n_layers, 2, A_N_HEADS, A_HEAD_DIM, ns), F32))
    res = pl.pallas_call(
        functools.partial(_ssd_kernel, seq_len=seq_len, has_h0=h0 is not None, emit_state=emit_state,
                          n_unread=n_unread),
        grid=(n_seq, A_N_GROUPS),
        in_specs=in_specs,
        out_specs=out_specs,
        out_shape=out_shape,
        scratch_shapes=[pltpu.VMEM((2, ns, gd), F32)],
        input_output_aliases=aliases,
        compiler_params=_cparams(2),
        name="ssd_core",
    )(*args)
    return res if emit_state else (res[0], None)


def _ssd_mixer(y, mod, cond_of_tile, w_in, conv_w, conv_b, dt_bias, a_log, d_skip, layer, state0, states_prev,
               n_prompt, len_prompt, n_latent, len_latent):
    m = y.shape[0]
    mp = n_prompt * len_prompt
    proj = _mod_linear(y, mod, w_in, cond_of_tile)
    xbc = _conv_silu(proj, A_D_INNER, A_CONV_DIM, conv_w, conv_b, mp, len_prompt, len_latent)
    hpg = A_HEADS_PER_GROUP
    dt0 = A_D_INNER + A_CONV_DIM
    dt_raw = proj[:, dt0:dt0 + 2 * A_N_HEADS].astype(F32)
    dt_col = dt_raw.reshape(m, 2, A_N_GROUPS, hpg).transpose(2, 0, 1, 3).reshape(A_N_GROUPS, m, 2 * hpg)
    dt_row = dt_raw.reshape(m // A_CHUNK, A_CHUNK, 2, A_N_GROUPS, hpg).transpose(3, 0, 2, 4, 1)
    dt_row = dt_row.reshape(A_N_GROUPS, m // A_CHUNK, 2 * hpg, A_CHUNK)

    def per_group(p):
        return p.reshape(2, A_N_GROUPS, hpg).transpose(1, 0, 2).reshape(A_N_GROUPS, 2 * hpg)

    bias_g = per_group(dt_bias.astype(F32))
    alog_g = per_group(a_log.astype(F32))
    dskip = jnp.repeat(d_skip.astype(F32), A_HEAD_DIM).reshape(A_N_GROUPS, 1, A_GROUP_DIM)
    params = (bias_g[:, None, :], bias_g[:, :, None], alog_g[:, None, :], alog_g[:, :, None], dskip)
    n_layers = state0.shape[1]
    out, st = _ssd_core(proj, xbc, dt_col, dt_row, params, layer, n_layers, None, None, states_prev, 0, n_prompt,
                        len_prompt, True)
    out, _ = _ssd_core(proj, xbc, dt_col, dt_row, params, layer, n_layers, state0.astype(F32), out, None, mp,
                       n_latent, len_latent, False)
    return out, st


def _gdn_kernel(*refs, seq_len, has_s0, emit_state):
    q_ref, k_ref, v_ref, z_ref, ab_ref, abt_ref, dtb_ref, alog_ref, ng_ref = refs[:9]
    pos = 9
    s0_ref = None
    if has_s0:
        s0_ref = refs[pos]
        pos += 2
    o_ref = refs[pos]
    pos += 1
    sfin_ref = None
    if emit_state:
        sfin_ref = refs[pos]
        pos += 1
    gq_scr, h_scr, et_scr, s_scr = refs[pos:pos + 4]

    hps = GDN_HEADS_PER_STEP
    hk, hv = B_HEAD_K, B_HEAD_V
    head0 = pl.program_id(1) * hps
    ch = B_CHUNK
    nc = seq_len // ch
    ii = lax.broadcasted_iota(jnp.int32, (ch, ch), 0)
    jj = lax.broadcasted_iota(jnp.int32, (ch, ch), 1)
    same_block = (ii // GDN_SOLVE_BLOCK) == (jj // GDN_SOLVE_BLOCK)
    dirs = []
    for d in range(2):
        dirs.append(dict(
            dt_bias=[dtb_ref[d, head0 + hh] for hh in range(hps)],
            a_neg=[-jnp.exp(alog_ref[d, head0 + hh]) for hh in range(hps)],
            incl=(ii >= jj) if d == 0 else (ii <= jj),
            incl_t=(ii <= jj) if d == 0 else (ii >= jj),
            strict=(ii > jj) if d == 0 else (ii < jj)))

    def mm(a, b):
        return _dot(a.astype(BF16), b.astype(BF16))

    def compose(a, b):
        return a + b + mm(a, b)

    n_prep = min(GDN_PREP_UNROLL, nc)

    def prepare(grp, carry):
        chains = []
        for j, hh in [(j, hh) for j in range(n_prep) for hh in range(hps)]:
            c = grp * n_prep + j
            rows = pl.ds(pl.multiple_of(c * ch, ch), ch)
            qc = q_ref[rows, hh * hk:(hh + 1) * hk].astype(F32)
            kc = k_ref[rows, hh * hk:(hh + 1) * hk].astype(F32)
            vc = v_ref[rows, hh * hv:(hh + 1) * hv].astype(F32)
            qn = qc * lax.rsqrt(jnp.sum(qc * qc, axis=-1, keepdims=True) + RMS_EPS) * (B_HEAD_K ** -0.5)
            kn = kc * lax.rsqrt(jnp.sum(kc * kc, axis=-1, keepdims=True) + RMS_EPS)
            knb = kn.astype(BF16)
            qk = _dot_nt(qn.astype(BF16), knb)
            kk = _dot_nt(knb, knb)
            ab = ab_ref[hh, rows, :]
            abt = abt_ref[hh, c]
            for d in range(2):
                p = dirs[d]
                a_neg, dt_bias = p["a_neg"][hh], p["dt_bias"][hh]
                beta = _sigmoid(ab[:, d:d + 1])
                g_col = a_neg * _softplus(ab[:, 2 + d:3 + d] + dt_bias)
                g_row = a_neg * _softplus(abt[2 + d:3 + d, :] + dt_bias)
                gcs = jnp.sum(jnp.where(p["incl"], jnp.broadcast_to(g_row, (ch, ch)), 0.0), axis=1, keepdims=True)
                gcs_t = jnp.sum(jnp.where(p["incl_t"], jnp.broadcast_to(g_col, (ch, ch)), 0.0), axis=0,
                                keepdims=True)
                tot = gcs[ch - 1:ch, :] if d == 0 else gcs[0:1, :]
                decay = jnp.exp(jnp.where(p["incl"], gcs - gcs_t, -jnp.inf))
                a_mat = jnp.where(p["strict"], beta * kk * decay, 0.0)
                a_diag = jnp.where(same_block, a_mat, 0.0)
                e_gcs = jnp.exp(gcs)
                chains.append(dict(
                    d=d, c=c, hh=hh, rows=rows, a_off=a_mat - a_diag, dx=-a_diag, pw=a_diag,
                    wu=jnp.concatenate([kn * (beta * e_gcs), vc * beta], axis=1),
                    kd=kn * jnp.exp(tot - gcs), qd=qn * e_gcs, qkm=(qk * decay).astype(BF16),
                    et=jnp.broadcast_to(jnp.exp(tot), (1, B_HEAD_V))))
        for _ in range(3):
            for t in chains:
                t["pw"] = mm(t["pw"], t["pw"])
            for t in chains:
                t["dx"] = compose(t["dx"], t["pw"])
        for t in chains:
            t["n"] = t["a_off"] + mm(t["dx"], t["a_off"])
        for t in chains:
            t["nn"] = mm(t["n"], t["n"])
        for t in chains:
            t["mx"] = compose(-t["n"], t["nn"])
        for t in chains:
            t["xb"] = compose(t["mx"], t["dx"]).astype(BF16)
        for t in chains:
            t["wu"] = (t["wu"] + _dot(t["xb"], t["wu"].astype(BF16))).astype(BF16)
        for t in chains:
            t["gh"] = _dot(jnp.transpose(t["kd"]).astype(BF16), t["wu"])
            t["qo"] = _dot(t["qkm"], t["wu"])
        for t in chains:
            d, c, hh = t["d"], t["c"], t["hh"]
            gq_scr[d, hh, c, 0:hk, :] = t["gh"][:, :hk].astype(BF16)
            gq_scr[d, hh, c, hk:, :] = (t["qd"] - t["qo"][:, :hk]).astype(BF16)
            h_scr[d, hh, c] = t["gh"][:, hk:]
            et_scr[d, hh, c] = t["et"]
        for fwd, bwd in zip(chains[0::2], chains[1::2]):
            hh = fwd["hh"]
            o_ref[fwd["rows"], hh * hv:(hh + 1) * hv] = fwd["qo"][:, hk:] + bwd["qo"][:, hk:]
        return carry

    lax.fori_loop(0, nc // n_prep, prepare, 0)

    for d in range(2):
        for hh in range(hps):
            if has_s0:
                s_scr[d, hh] = s0_ref[d, hh]
            else:
                s_scr[d, hh] = jnp.zeros((hk, hv), F32)

    def recur(ci, carry):
        olds = {}
        prods = {}
        for d in range(2):
            c = ci if d == 0 else nc - 1 - ci
            for hh in range(hps):
                olds[d, hh] = s_scr[d, hh]
                prods[d, hh] = _dot(gq_scr[d, hh, c], olds[d, hh].astype(BF16))
        for d in range(2):
            c = ci if d == 0 else nc - 1 - ci
            rows = pl.ds(pl.multiple_of(c * ch, ch), ch)
            for hh in range(hps):
                r = prods[d, hh]
                s_scr[d, hh] = olds[d, hh] * et_scr[d, hh, c] - r[:hk] + h_scr[d, hh, c]
                o_ref[rows, hh * hv:(hh + 1) * hv] += r[hk:]
        return carry

    lax.fori_loop(0, nc, recur, 0)
    if emit_state:
        for d in range(2):
            for hh in range(hps):
                sfin_ref[d, hh] = s_scr[d, hh]

    for hh in range(hps):
        cols = slice(hh * hv, (hh + 1) * hv)
        o = o_ref[:, cols]
        o = o * lax.rsqrt(jnp.mean(o * o, axis=-1, keepdims=True) + RMS_EPS) * ng_ref[...]
        o_ref[:, cols] = o * _silu(z_ref[:, cols].astype(F32))


def _gdn_core(proj, qkv, ab_col, ab_row, dt_bias, a_log, norm_g, s0, prev, row0, n_seq, seq_len, emit_state):
    m = proj.shape[0]
    hk, hv = B_HEAD_K, B_HEAD_V
    hps = GDN_HEADS_PER_STEP
    wk, wv = hps * hk, hps * hv
    sb = row0 // seq_len
    nc = seq_len // B_CHUNK
    in_specs = [
        pl.BlockSpec((seq_len, wk), lambda n, h: (sb + n, h)),
        pl.BlockSpec((seq_len, wk), lambda n, h: (sb + n, B_QK_DIM // wk + h)),
        pl.BlockSpec((seq_len, wv), lambda n, h: (sb + n, (2 * B_QK_DIM) // wv + h)),
        pl.BlockSpec((seq_len, wv), lambda n, h: (sb + n, B_CONV_DIM // wv + h)),
        pl.BlockSpec((hps, seq_len, 4), lambda n, h: (h, sb + n, 0)),
        pl.BlockSpec((hps, nc, 4, B_CHUNK), lambda n, h: (h, sb + n, 0, 0)),
        pl.BlockSpec(memory_space=pltpu.SMEM),
        pl.BlockSpec(memory_space=pltpu.SMEM),
        pl.BlockSpec((1, hv), lambda n, h: (0, 0)),
    ]
    args = [qkv, qkv, qkv, proj, ab_col, ab_row, dt_bias, a_log, norm_g.reshape(1, hv)]
    aliases = {}
    if s0 is not None:
        in_specs += [pl.BlockSpec((None, 2, hps, hk, hv), lambda n, h: (n, 0, h, 0, 0)),
                     pl.BlockSpec(memory_space=pl.ANY)]
        args += [s0, prev]
        aliases = {len(args) - 1: 0}
    out_specs = [pl.BlockSpec((seq_len, wv), lambda n, h: (sb + n, h))]
    out_shape = [jax.ShapeDtypeStruct((m, B_V_DIM), F32)]
    if emit_state:
        out_specs.append(pl.BlockSpec((None, 2, hps, hk, hv), lambda n, h: (n, 0, h, 0, 0)))
        out_shape.append(jax.ShapeDtypeStruct((n_seq, 2, B_N_HEADS, hk, hv), F32))
    res = pl.pallas_call(
        functools.partial(_gdn_kernel, seq_len=seq_len, has_s0=s0 is not None, emit_state=emit_state),
        grid=(n_seq, B_N_HEADS // hps),
        in_specs=in_specs,
        out_specs=out_specs,
        out_shape=out_shape,
        scratch_shapes=[
            pltpu.VMEM((2, hps, nc, hk + B_CHUNK, hk), BF16),
            pltpu.VMEM((2, hps, nc, hk, hv), F32),
            pltpu.VMEM((2, hps, nc, 1, hv), F32),
            pltpu.VMEM((2, hps, hk, hv), F32),
        ],
        input_output_aliases=aliases,
        compiler_params=_cparams(2),
        name="gdn_core",
    )(*args)
    return res if emit_state else (res[0], None)


def _gdn_mixer(y, mod, cond_of_tile, w_in, conv_w, conv_b, dt_bias, a_log, norm_g, state0, n_prompt, len_prompt,
               n_latent, len_latent):
    m = y.shape[0]
    mp = n_prompt * len_prompt
    proj = _mod_linear(y, mod, w_in, cond_of_tile)
    qkv = _conv_silu(proj, 0, B_CONV_DIM, conv_w, conv_b, mp, len_prompt, len_latent)
    ab0 = B_CONV_DIM + B_V_DIM
    ab = proj[:, ab0:ab0 + 4 * B_N_HEADS].astype(F32).reshape(m, 4, B_N_HEADS)
    ab_col = ab.transpose(2, 0, 1)
    ab_row = ab.reshape(m // B_CHUNK, B_CHUNK, 4, B_N_HEADS).transpose(3, 0, 2, 1)
    dt_bias = dt_bias.astype(F32)
    a_log = a_log.astype(F32)
    out, st = _gdn_core(proj, qkv, ab_col, ab_row, dt_bias, a_log, norm_g, None, None, 0, n_prompt, len_prompt, True)
    out, _ = _gdn_core(proj, qkv, ab_col, ab_row, dt_bias, a_log, norm_g, state0, out, mp, n_latent, len_latent,
                       False)
    return out, st


KV_PAIR = 2
PAIR_Q = KV_PAIR * C_GROUP * C_HEAD_DIM
PAIR_KV = KV_PAIR * C_HEAD_DIM


def _softmax_pv(heads, values):
    mxs = []
    for scores, sink in heads:
        mx = sink
        for s in scores:
            mx = jnp.maximum(mx, jnp.max(s, axis=-1, keepdims=True))
        mxs.append(mx)
    probs = [[jnp.exp(s - mx) for s in scores] for (scores, _), mx in zip(heads, mxs)]
    dens = []
    for (_, sink), mx, ps in zip(heads, mxs, probs):
        den = jnp.exp(sink - mx)
        for p in ps:
            den = den + jnp.sum(p, axis=-1, keepdims=True)
        dens.append(den)
    outs = []
    for ps in probs:
        acc = None
        for p, v in zip(ps, values):
            pv = _dot(p.astype(BF16), v)
            acc = pv if acc is None else acc + pv
        outs.append(acc)
    return [acc / den for acc, den in zip(outs, dens)]


def _attn_ctx_kernel(q_ref, k_ref, v_ref, sink_ref, o_ref):
    pair = pl.program_id(1)
    hd = C_HEAD_DIM
    for gg in range(KV_PAIR):
        kb = k_ref[:, gg * hd:(gg + 1) * hd].astype(BF16)
        vb = v_ref[:, gg * hd:(gg + 1) * hd].astype(BF16)
        heads = []
        for r in range(C_GROUP):
            col = (gg * C_GROUP + r) * hd
            qb = q_ref[:, col:col + hd].astype(BF16)
            sink = sink_ref[pair * (KV_PAIR * C_GROUP) + gg * C_GROUP + r]
            heads.append(([_dot_nt(qb, kb) * C_SCALE], sink))
        for r, o in enumerate(_softmax_pv(heads, [vb])):
            col = (gg * C_GROUP + r) * hd
            o_ref[:, col:col + hd] = o


def _attn_ctx(proj, sink, n_prompt, len_prompt):
    m = proj.shape[0]
    k_block0 = C_Q_DIM // PAIR_KV
    v_block0 = (C_Q_DIM + C_KV_DIM) // PAIR_KV
    return pl.pallas_call(
        _attn_ctx_kernel,
        grid=(n_prompt, C_N_KV // KV_PAIR),
        in_specs=[
            pl.BlockSpec((len_prompt, PAIR_Q), lambda n, p: (n, p)),
            pl.BlockSpec((len_prompt, PAIR_KV), lambda n, p: (n, k_block0 + p)),
            pl.BlockSpec((len_prompt, PAIR_KV), lambda n, p: (n, v_block0 + p)),
            pl.BlockSpec(memory_space=pltpu.SMEM),
        ],
        out_specs=pl.BlockSpec((len_prompt, PAIR_Q), lambda n, p: (n, p)),
        out_shape=jax.ShapeDtypeStruct((m, C_Q_DIM), F32),
        compiler_params=_cparams(2),
        name="attn_ctx",
    )(proj, proj, proj, sink)


def _rope(x, cos, sin):
    width = x.shape[1]
    lane = lax.broadcasted_iota(jnp.int32, x.shape, 1)
    nf = C_HEAD_DIM // 4
    partner = jnp.where(lane % (2 * nf) < nf, pltpu.roll(x, width - nf, 1), pltpu.roll(x, nf, 1))
    return x * cos + partner * sin


def _attn_lat_kernel(q_ref, kp_ref, vp_ref, kc_ref, vc_ref, cq_ref, sq_ref, ck_ref, sk_ref, sink_ref, prev_ref, o_ref,
                     *, seq_len):
    del prev_ref
    pair = pl.program_id(1)
    qi = pl.program_id(2)
    hd = C_HEAD_DIM
    blk = C_BLOCK
    span = C_BLOCK + 2 * C_WINDOW
    start = pl.multiple_of(qi * blk, blk)
    q = _rope(q_ref[...].astype(F32), cq_ref[...], sq_ref[...])
    kw = _rope(kp_ref[pl.ds(start, span), :].astype(F32), ck_ref[pl.ds(start, span), :],
               sk_ref[pl.ds(start, span), :])
    vw = vp_ref[pl.ds(start, span), :]
    iq = lax.broadcasted_iota(jnp.int32, (blk, span), 0)
    jk = lax.broadcasted_iota(jnp.int32, (blk, span), 1)
    kpos = start - C_WINDOW + jk
    ok = jnp.logical_and(jnp.abs(iq + C_WINDOW - jk) <= C_WINDOW, jnp.logical_and(kpos >= 0, kpos < seq_len))
    for gg in range(KV_PAIR):
        kb = kw[:, gg * hd:(gg + 1) * hd].astype(BF16)
        vb = vw[:, gg * hd:(gg + 1) * hd].astype(BF16)
        kcb = kc_ref[:, gg * hd:(gg + 1) * hd].astype(BF16)
        vcb = vc_ref[:, gg * hd:(gg + 1) * hd].astype(BF16)
        heads = []
        for r in range(C_GROUP):
            col = (gg * C_GROUP + r) * hd
            qb = q[:, col:col + hd].astype(BF16)
            s_loc = jnp.where(ok, _dot_nt(qb, kb) * C_SCALE, -jnp.inf)
            s_ctx = _dot_nt(qb, kcb) * C_SCALE
            sink = sink_ref[pair * (KV_PAIR * C_GROUP) + gg * C_GROUP + r]
            heads.append(([s_loc, s_ctx], sink))
        for r, o in enumerate(_softmax_pv(heads, [vb, vcb])):
            col = (gg * C_GROUP + r) * hd
            o_ref[:, col:col + hd] = o


def _rope_tables(seq_len):
    nf = C_HEAD_DIM // 4
    t = jnp.arange(seq_len)
    inv_freq = ROPE_BASE ** (-jnp.arange(nf, dtype=F32) / nf)
    ang_r = (t // GRID_W).astype(F32)[:, None] * inv_freq
    ang_c = (t % GRID_W).astype(F32)[:, None] * inv_freq
    cos = jnp.concatenate([jnp.cos(ang_r), jnp.cos(ang_r), jnp.cos(ang_c), jnp.cos(ang_c)], axis=1)
    sin = jnp.concatenate([-jnp.sin(ang_r), jnp.sin(ang_r), -jnp.sin(ang_c), jnp.sin(ang_c)], axis=1)
    return cos, sin


def _attn_lat(proj, cache_k, cache_v, sink, prev, row0, n_latent, len_latent):
    m = proj.shape[0]
    past = cache_k.shape[1]
    w = C_WINDOW
    k_lat = proj[row0:, C_Q_DIM:C_Q_DIM + C_KV_DIM].reshape(n_latent, len_latent, C_KV_DIM)
    v_lat = proj[row0:, C_Q_DIM + C_KV_DIM:C_Q_DIM + 2 * C_KV_DIM].reshape(n_latent, len_latent, C_KV_DIM)
    kp = jnp.pad(k_lat, ((0, 0), (w, w), (0, 0)))
    vp = jnp.pad(v_lat, ((0, 0), (w, w), (0, 0)))
    kc = cache_k.reshape(n_latent, past, C_KV_DIM).astype(F32)
    vc = cache_v.reshape(n_latent, past, C_KV_DIM).astype(F32)
    cos, sin = _rope_tables(len_latent)
    cos_q = jnp.tile(cos, (1, PAIR_Q // C_HEAD_DIM))
    sin_q = jnp.tile(sin, (1, PAIR_Q // C_HEAD_DIM))
    cos_k = jnp.pad(jnp.tile(cos, (1, KV_PAIR)), ((w, w), (0, 0)))
    sin_k = jnp.pad(jnp.tile(sin, (1, KV_PAIR)), ((w, w), (0, 0)))
    qb0 = row0 // C_BLOCK
    nqb = len_latent // C_BLOCK
    padded = len_latent + 2 * w
    return pl.pallas_call(
        functools.partial(_attn_lat_kernel, seq_len=len_latent),
        grid=(n_latent, C_N_KV // KV_PAIR, nqb),
        in_specs=[
            pl.BlockSpec((C_BLOCK, PAIR_Q), lambda n, p, i: (qb0 + n * nqb + i, p)),
            pl.BlockSpec((None, padded, PAIR_KV), lambda n, p, i: (n, 0, p)),
            pl.BlockSpec((None, padded, PAIR_KV), lambda n, p, i: (n, 0, p)),
            pl.BlockSpec((None, past, PAIR_KV), lambda n, p, i: (n, 0, p)),
            pl.BlockSpec((None, past, PAIR_KV), lambda n, p, i: (n, 0, p)),
            pl.BlockSpec((C_BLOCK, PAIR_Q), lambda n, p, i: (i, 0)),
            pl.BlockSpec((C_BLOCK, PAIR_Q), lambda n, p, i: (i, 0)),
            pl.BlockSpec((padded, PAIR_KV), lambda n, p, i: (0, 0)),
            pl.BlockSpec((padded, PAIR_KV), lambda n, p, i: (0, 0)),
            pl.BlockSpec(memory_space=pltpu.SMEM),
            pl.BlockSpec(memory_space=pl.ANY),
        ],
        out_specs=pl.BlockSpec((C_BLOCK, PAIR_Q), lambda n, p, i: (qb0 + n * nqb + i, p)),
        out_shape=jax.ShapeDtypeStruct((m, C_Q_DIM), F32),
        input_output_aliases={10: 0},
        compiler_params=_cparams(3),
        name="attn_lat",
    )(proj, kp, vp, kc, vc, cos_q, sin_q, cos_k, sin_k, sink, prev)


def _attn_mixer(y, mod, cond_of_tile, w_in, sink, cache_k, cache_v, n_prompt, len_prompt, n_latent, len_latent):
    m = y.shape[0]
    mp = n_prompt * len_prompt
    proj = _mod_linear(y, mod, w_in, cond_of_tile)
    sink = sink.astype(F32)
    out = _attn_ctx(proj, sink, n_prompt, len_prompt)
    out = _attn_lat(proj, cache_k, cache_v, sink, out, mp, n_latent, len_latent)
    new_k = proj[:mp, C_Q_DIM:C_Q_DIM + C_KV_DIM].astype(F32).reshape(n_prompt, len_prompt, C_N_KV, C_HEAD_DIM)
    new_v = proj[:mp, C_Q_DIM + C_KV_DIM:C_Q_DIM + 2 * C_KV_DIM].astype(F32)
    new_v = new_v.reshape(n_prompt, len_prompt, C_N_KV, C_HEAD_DIM)
    return out, new_k, new_v


def kernel(x_prompt, x_sample, state_ssd, state_delta, cache_k, cache_v, c, c_ctx, w_mod, b_mod, ln_g, ln_b, ffn_w_gate, ffn_w_up, ffn_w_down, ssd_w_in, ssd_conv_w, ssd_conv_b, ssd_dt_bias, ssd_a_log, ssd_d, ssd_norm, ssd_w_out, gdn_w_in, gdn_conv_w, gdn_conv_b, gdn_dt_bias, gdn_a_log, gdn_norm, gdn_w_out, attn_w_in, attn_sink, attn_w_out):
    n_prompt, len_prompt, d = x_prompt.shape
    n_latent, len_latent, _ = x_sample.shape
    mp = n_prompt * len_prompt
    assert d == D_MODEL and n_latent + 1 <= N_COND
    assert len_prompt % ROW_TILE == 0 or ROW_TILE % len_prompt == 0
    assert mp % ROW_TILE == 0 and len_latent % ROW_TILE == 0
    assert mp % len_latent == 0 and len_latent % len_prompt == 0 and len_latent % GRID_W == 0

    n_prompt_tiles = mp // ROW_TILE
    tiles_per_request = len_latent // ROW_TILE
    cond_of_tile = functools.partial(_cond_index, n_prompt_tiles=n_prompt_tiles, tiles_per_request=tiles_per_request)

    cond = jnp.concatenate([c_ctx[None].astype(F32), c.astype(F32),
                            jnp.zeros((N_COND - 1 - n_latent, d), F32)], axis=0)
    mods = _adaln(cond, w_mod, b_mod).reshape(DEPTH, N_COND, N_MOD, d)

    y = (x_prompt.reshape(mp, d).astype(F32), x_sample.reshape(n_latent * len_latent, d).astype(F32))
    w_gate, w_up, w_down = ffn_w_gate.astype(BF16), ffn_w_up.astype(BF16), ffn_w_down.astype(BF16)
    seqs = (n_prompt, len_prompt, n_latent, len_latent)
    ssd_states, gdn_states, k_list, v_list = None, [], [], []
    for i in range(DEPTH):
        mod = mods[i]
        y = _ffn_half(y, mod, 0, ln_g[i, 0], ln_b[i, 0], w_gate, w_up, w_down, i, 0, cond_of_tile, mp)
        kind, j = i % 3, i // 3
        if kind == 0:
            a, ssd_states = _ssd_mixer(y, mod, cond_of_tile, ssd_w_in[j], ssd_conv_w[j], ssd_conv_b[j],
                                       ssd_dt_bias[j], ssd_a_log[j], ssd_d[j], j, state_ssd, ssd_states, *seqs)
            y = _out_proj_ln(a, y, mod, ssd_w_out[j], ln_g[i, 1], ln_b[i, 1], cond_of_tile, rms_g=ssd_norm[j])
        elif kind == 1:
            a, st = _gdn_mixer(y, mod, cond_of_tile, gdn_w_in[j], gdn_conv_w[j], gdn_conv_b[j], gdn_dt_bias[j],
                               gdn_a_log[j], gdn_norm[j], state_delta[:, j], *seqs)
            gdn_states.append(st)
            y = _out_proj_ln(a, y, mod, gdn_w_out[j], ln_g[i, 1], ln_b[i, 1], cond_of_tile)
        else:
            a, kc, vc = _attn_mixer(y, mod, cond_of_tile, attn_w_in[j], attn_sink[j], cache_k[:, j], cache_v[:, j],
                                    *seqs)
            k_list.append(kc)
            v_list.append(vc)
            y = _out_proj_ln(a, y, mod, attn_w_out[j], ln_g[i, 1], ln_b[i, 1], cond_of_tile)
        y = _ffn_half(y, mod, 2, ln_g[i, 2], ln_b[i, 2], w_gate, w_up, w_down, i, 1, cond_of_tile, mp,
                      split_out=i == DEPTH - 1)

    y_prompt = y[0].reshape(n_prompt, len_prompt, d)
    y_sample = y[1].reshape(n_latent, len_latent, d)
    return (y_prompt, y_sample, ssd_states, jnp.stack(gdn_states, axis=1),
            jnp.stack(k_list, axis=1), jnp.stack(v_list, axis=1))
```

```python
import functools
import math

import jax
import jax.numpy as jnp
from jax import lax
from jax.experimental import pallas as pl
from jax.experimental.pallas import tpu as pltpu

F32 = jnp.float32
BF16 = jnp.bfloat16

D_MODEL = 1024
DEPTH = 4
GRID_W = 64
N_MOD = 9
D_FF = 2816
DEEPNORM_ALPHA = (2 * DEPTH) ** 0.25
LN_EPS = 1e-5
RMS_EPS = 1e-6
FFN_RES = 0.5

A_D_INNER = 2 * D_MODEL
A_HEAD_DIM = 64
A_N_HEADS = A_D_INNER // A_HEAD_DIM
A_N_GROUPS = 4
A_HEADS_PER_GROUP = A_N_HEADS // A_N_GROUPS
A_GROUP_DIM = A_HEADS_PER_GROUP * A_HEAD_DIM
A_D_STATE = 128
A_CHUNK = 128
A_CONV_DIM = A_D_INNER + 2 * A_N_GROUPS * A_D_STATE

B_N_HEADS = 8
B_HEAD_K = 128
B_HEAD_V = 256
B_CHUNK = 64
B_QK_DIM = B_N_HEADS * B_HEAD_K
B_V_DIM = B_N_HEADS * B_HEAD_V
B_CONV_DIM = 2 * B_QK_DIM + B_V_DIM
GDN_SOLVE_BLOCK = 16
GDN_PREP_UNROLL = 4
GDN_HEADS_PER_STEP = 2

C_N_HEADS = 16
C_N_KV = 4
C_GROUP = C_N_HEADS // C_N_KV
C_HEAD_DIM = 64
C_WINDOW = 128
C_BLOCK = 128
C_Q_DIM = C_N_HEADS * C_HEAD_DIM
C_KV_DIM = C_N_KV * C_HEAD_DIM
C_SCALE = C_HEAD_DIM ** -0.5
ROPE_BASE = 10000.0

N_COND = 16
ROW_TILE = 512
PROJ_TILE_MAX = 3328
PROJ_ALIGN = 256
FF_TILE = 1408
FFN_ROW_TILE = 512
FFN_SUBTILES = 2
CONV_KERNEL = 5
CONV_PAD = 16
CONV_COLS = 256
VMEM_LIMIT = 56 * 1024 * 1024


def _cparams(n_axes):
    return pltpu.CompilerParams(dimension_semantics=("arbitrary",) * n_axes, vmem_limit_bytes=VMEM_LIMIT)


def _dot(a, b, precision=None):
    return lax.dot_general(a, b, (((1,), (0,)), ((), ())), precision=precision, preferred_element_type=F32)


def _dot_nt(a, b, precision=None):
    return lax.dot_general(a, b, (((1,), (1,)), ((), ())), precision=precision, preferred_element_type=F32)


def _dot_tn(a, b, precision=None):
    return lax.dot_general(a, b, (((0,), (0,)), ((), ())), precision=precision, preferred_element_type=F32)


def _split3(x):
    hi = x.astype(BF16)
    rest = x - hi.astype(F32)
    mid = rest.astype(BF16)
    lo = (rest - mid.astype(F32)).astype(BF16)
    return [hi, mid, lo]


def _sigmoid(x):
    return 1.0 / (1.0 + jnp.exp(-x))


def _silu(x):
    return x * _sigmoid(x)


def _softplus(x):
    return jnp.maximum(x, 0.0) + jnp.log(1.0 + jnp.exp(-jnp.abs(x)))


def _layer_norm(t, g, b):
    mu = jnp.mean(t, axis=-1, keepdims=True)
    tc = t - mu
    var = jnp.mean(tc * tc, axis=-1, keepdims=True)
    return tc * lax.rsqrt(var + LN_EPS) * g + b


def _cond_index(i, n_prompt_tiles, tiles_per_request):
    return jnp.where(i < n_prompt_tiles, 0, 1 + (jnp.maximum(i - n_prompt_tiles, 0)) // tiles_per_request)


def _adaln_kernel(c_ref, w_ref, b_ref, o_ref):
    c = c_ref[...]
    h = _silu(c).astype(BF16)
    o_ref[...] = _dot(h, w_ref[...].astype(BF16)) + b_ref[...]


def _adaln(cond, w_mod, b_mod):
    n_out = N_MOD * D_MODEL
    tn = D_MODEL
    return pl.pallas_call(
        _adaln_kernel,
        grid=(DEPTH, n_out // tn),
        in_specs=[
            pl.BlockSpec((N_COND, D_MODEL), lambda l, j: (0, 0)),
            pl.BlockSpec((None, D_MODEL, tn), lambda l, j: (l, 0, j)),
            pl.BlockSpec((None, 1, tn), lambda l, j: (l, 0, j)),
        ],
        out_specs=pl.BlockSpec((None, N_COND, tn), lambda l, j: (l, 0, j)),
        out_shape=jax.ShapeDtypeStruct((DEPTH, N_COND, n_out), F32),
        compiler_params=_cparams(2),
        name="adaln",
    )(cond, w_mod, b_mod.reshape(DEPTH, 1, n_out))


def _ffn_kernel(*refs, s, n_in, n_out, n_prompt_tiles):
    y_refs = refs[:n_in]
    mod_ref, g_ref, b_ref, wg_ref, wu_ref, wd_ref = refs[n_in:n_in + 6]
    o_refs = refs[n_in + 6:]
    is_prompt = pl.program_id(0) < n_prompt_tiles
    shift = mod_ref[pl.ds(3 * s, 1), :]
    scale = mod_ref[pl.ds(3 * s + 1, 1), :]
    gate = mod_ref[pl.ds(3 * s + 2, 1), :]
    sub = FFN_ROW_TILE // FFN_SUBTILES
    for r in range(FFN_SUBTILES):
        rows = slice(r * sub, (r + 1) * sub)
        if n_in == 1:
            y = y_refs[0][rows, :]
        else:
            y = jnp.where(is_prompt, y_refs[0][rows, :], y_refs[1][rows, :])
        h = (y * (1.0 + scale) + shift).astype(BF16)
        acc = None
        for c in range(D_FF // FF_TILE):
            cols = slice(c * FF_TILE, (c + 1) * FF_TILE)
            a = _dot(h, wg_ref[:, cols])
            u = _dot(h, wu_ref[:, cols])
            f = (_silu(a) * u).astype(BF16)
            part = _dot(f, wd_ref[cols, :])
            acc = part if acc is None else acc + part
        t = DEEPNORM_ALPHA * y + (FFN_RES * gate) * acc
        res = _layer_norm(t, g_ref[...], b_ref[...])
        if n_out == 1:
            o_refs[0][rows, :] = res
        else:
            @pl.when(is_prompt)
            def _(res=res, rows=rows):
                o_refs[0][rows, :] = res

            @pl.when(jnp.logical_not(is_prompt))
            def _(res=res, rows=rows):
                o_refs[1][rows, :] = res


def _ffn_half(ys, mod, s, g, b, w_gate, w_up, w_down, layer, half, cond_of_tile, n_prompt_rows, split_out=False):
    npt = n_prompt_rows // FFN_ROW_TILE
    split_in = isinstance(ys, (tuple, list))
    ys = tuple(ys) if split_in else (ys,)
    m = sum(a.shape[0] for a in ys)
    resident = pl.Buffered(1)

    def whole(i):
        return (i, 0)

    def prompt_part(i):
        return (jnp.minimum(i, npt - 1), 0)

    def latent_part(i):
        return (jnp.maximum(i - npt, 0), 0)

    def row_specs(split):
        return [pl.BlockSpec((FFN_ROW_TILE, D_MODEL), f)
                for f in ((prompt_part, latent_part) if split else (whole,))]

    if split_out:
        out_shape = [jax.ShapeDtypeStruct((n_prompt_rows, D_MODEL), F32),
                     jax.ShapeDtypeStruct((m - n_prompt_rows, D_MODEL), F32)]
    else:
        out_shape = [jax.ShapeDtypeStruct((m, D_MODEL), F32)]
    res = pl.pallas_call(
        functools.partial(_ffn_kernel, s=s, n_in=len(ys), n_out=len(out_shape), n_prompt_tiles=npt),
        grid=(m // FFN_ROW_TILE,),
        in_specs=row_specs(split_in) + [
            pl.BlockSpec((None, N_MOD, D_MODEL), lambda i: (cond_of_tile(i, FFN_ROW_TILE), 0, 0)),
            pl.BlockSpec((1, D_MODEL), lambda i: (0, 0)),
            pl.BlockSpec((1, D_MODEL), lambda i: (0, 0)),
            pl.BlockSpec((None, None, D_MODEL, D_FF), lambda i: (layer, half, 0, 0), pipeline_mode=resident),
            pl.BlockSpec((None, None, D_MODEL, D_FF), lambda i: (layer, half, 0, 0), pipeline_mode=resident),
            pl.BlockSpec((None, None, D_FF, D_MODEL), lambda i: (layer, half, 0, 0), pipeline_mode=resident),
        ],
        out_specs=row_specs(split_out),
        out_shape=out_shape,
        compiler_params=_cparams(1),
        name="ffn_half",
    )(*ys, mod, g.reshape(1, -1), b.reshape(1, -1), w_gate, w_up, w_down)
    return tuple(res) if split_out else res[0]


def _modlin_kernel(y_ref, mod_ref, w_ref, o_ref):
    shift = mod_ref[pl.ds(3, 1), :]
    scale = mod_ref[pl.ds(4, 1), :]
    h = (y_ref[...] * (1.0 + scale) + shift).astype(BF16)
    o_ref[...] = _dot(h, w_ref[...]).astype(o_ref.dtype)


def _mod_linear(y, mod, w, cond_of_tile):
    m = y.shape[0]
    n = w.shape[1]
    n_tiles = -(-n // PROJ_TILE_MAX)
    tn = -(-n // (n_tiles * PROJ_ALIGN)) * PROJ_ALIGN
    n_pad = n_tiles * tn
    w = jnp.pad(w.astype(BF16), ((0, 0), (0, n_pad - n)))
    return pl.pallas_call(
        _modlin_kernel,
        grid=(n_tiles, m // ROW_TILE),
        in_specs=[
            pl.BlockSpec((ROW_TILE, D_MODEL), lambda j, i: (i, 0)),
            pl.BlockSpec((None, N_MOD, D_MODEL), lambda j, i: (cond_of_tile(i), 0, 0)),
            pl.BlockSpec((D_MODEL, tn), lambda j, i: (0, j)),
        ],
        out_specs=pl.BlockSpec((ROW_TILE, tn), lambda j, i: (i, j)),
        out_shape=jax.ShapeDtypeStruct((m, n_pad), BF16),
        compiler_params=_cparams(2),
        name="mod_linear",
    )(y, mod, w)


def _outproj_kernel(a_ref, y_ref, mod_ref, w_ref, g_ref, b_ref, *rest, rms):
    if rms:
        ng_ref, o_ref = rest
    else:
        (o_ref,) = rest
    a = a_ref[...]
    if rms:
        a = a * lax.rsqrt(jnp.mean(a * a, axis=-1, keepdims=True) + RMS_EPS) * ng_ref[...]
    mix = _dot(a.astype(BF16), w_ref[...])
    gate = mod_ref[pl.ds(5, 1), :]
    t = DEEPNORM_ALPHA * y_ref[...] + gate * mix
    o_ref[...] = _layer_norm(t, g_ref[...], b_ref[...])


def _out_proj_ln(a, y, mod, w, g, b, cond_of_tile, rms_g=None):
    m, k = a.shape
    in_specs = [
        pl.BlockSpec((ROW_TILE, k), lambda i: (i, 0)),
        pl.BlockSpec((ROW_TILE, D_MODEL), lambda i: (i, 0)),
        pl.BlockSpec((None, N_MOD, D_MODEL), lambda i: (cond_of_tile(i), 0, 0)),
        pl.BlockSpec((k, D_MODEL), lambda i: (0, 0)),
        pl.BlockSpec((1, D_MODEL), lambda i: (0, 0)),
        pl.BlockSpec((1, D_MODEL), lambda i: (0, 0)),
    ]
    args = [a, y, mod, w.astype(BF16), g.reshape(1, -1), b.reshape(1, -1)]
    if rms_g is not None:
        in_specs.append(pl.BlockSpec((1, k), lambda i: (0, 0)))
        args.append(rms_g.reshape(1, -1))
    return pl.pallas_call(
        functools.partial(_outproj_kernel, rms=rms_g is not None),
        grid=(m // ROW_TILE,),
        in_specs=in_specs,
        out_specs=pl.BlockSpec((ROW_TILE, D_MODEL), lambda i: (i, 0)),
        out_shape=jax.ShapeDtypeStruct((m, D_MODEL), F32),
        compiler_params=_cparams(1),
        name="out_proj_ln",
    )(*args)


def _conv_kernel(x_ref, w_ref, b_ref, o_ref, xs_scr, *, n_prompt_tiles, len_prompt):
    rows = x_ref.shape[0]
    pad = CONV_PAD
    half = CONV_KERNEL // 2
    zeros = jnp.zeros((pad, xs_scr.shape[1]), F32)
    xs_scr[0:pad, :] = zeros
    xs_scr[pad + rows:, :] = zeros
    xs_scr[pad:pad + rows, :] = x_ref[...].astype(F32)

    def taps(start, n, valid=None):
        acc = b_ref[...]
        for k in range(CONV_KERNEL):
            xk = xs_scr[pl.ds(pad + start + k - half, n), :]
            if valid is not None:
                xk = jnp.where(valid[k], xk, 0.0)
            acc = acc + w_ref[pl.ds(k, 1), :] * xk
        return _silu(acc).astype(o_ref.dtype)

    o_ref[...] = taps(0, rows)

    @pl.when(pl.program_id(0) < n_prompt_tiles)
    def _():
        i = lax.broadcasted_iota(jnp.int32, (2 * pad, 1), 0)
        valid = [(i < pad) == (i + k - half < pad) for k in range(CONV_KERNEL)]
        for b in range(1, rows // len_prompt):
            r = b * len_prompt
            o_ref[r - pad:r + pad, :] = taps(r - pad, 2 * pad, valid)


def _conv_silu(proj, col0, width, conv_w, conv_b, n_prompt_rows, len_prompt, len_latent):
    m = proj.shape[0]
    rows = len_latent
    c0 = col0 // CONV_COLS
    return pl.pallas_call(
        functools.partial(_conv_kernel, n_prompt_tiles=n_prompt_rows // rows, len_prompt=len_prompt),
        grid=(m // rows, width // CONV_COLS),
        in_specs=[
            pl.BlockSpec((rows, CONV_COLS), lambda i, j: (i, c0 + j)),
            pl.BlockSpec((CONV_KERNEL, CONV_COLS), lambda i, j: (0, j)),
            pl.BlockSpec((1, CONV_COLS), lambda i, j: (0, j)),
        ],
        out_specs=pl.BlockSpec((rows, CONV_COLS), lambda i, j: (i, j)),
        out_shape=jax.ShapeDtypeStruct((m, width), BF16),
        scratch_shapes=[pltpu.VMEM((rows + 2 * CONV_PAD, CONV_COLS), F32)],
        compiler_params=_cparams(2),
        name="conv_silu",
    )(proj, conv_w, conv_b.reshape(1, -1))


def _ssd_kernel(*refs, seq_len, has_h0, emit_state, n_unread):
    x_ref, z_ref, b_ref, c_ref, dtt_ref, biast_ref, alogt_ref, dskip_ref = refs[:8]
    pos = 8
    h0_ref = None
    if has_h0:
        h0_ref = refs[pos]
        pos += 1
    pos += n_unread
    y_ref = refs[pos]
    pos += 1
    hfin_ref = None
    if emit_state:
        hfin_ref = refs[pos]
        pos += 1
    s_scr, dtct_scr, acst_scr, dtw_scr, acsw_scr, acsb_scr = refs[pos:pos + 6]

    q = A_CHUNK
    nc = seq_len // q
    hpg = A_HEADS_PER_GROUP
    hd = A_HEAD_DIM
    ii = lax.broadcasted_iota(jnp.int32, (q, q), 0)
    jj = lax.broadcasted_iota(jnp.int32, (q, q), 1)
    lower = ii >= jj
    upper = ii <= jj
    t_lower = jnp.where(lower, 1.0, 0.0).astype(BF16)
    t_upper = jnp.where(upper, 1.0, 0.0).astype(BF16)
    t_lower_r = jnp.concatenate([t_lower] * 3, axis=0)
    t_upper_r = jnp.concatenate([t_upper] * 3, axis=0)

    def expander(width):
        er = lax.broadcasted_iota(jnp.int32, (3 * hpg, hpg * width), 0)
        ec = lax.broadcasted_iota(jnp.int32, (3 * hpg, hpg * width), 1)
        return jnp.where(ec // width == er % hpg, 1.0, 0.0).astype(BF16)

    expand_hd = expander(hd)
    expand_q = expander(q)
    first_head = lax.broadcasted_iota(jnp.int32, (q, 2 * hd), 1) < hd

    def pieces(v_t):
        return jnp.concatenate([p.astype(F32) for p in _split3(v_t)], axis=0).astype(BF16)

    y_ref[...] = dskip_ref[...] * x_ref[...].astype(F32)

    for d in range(2):
        bias_t = biast_ref[d * hpg:(d + 1) * hpg, :]
        a_neg_t = -jnp.exp(alogt_ref[d * hpg:(d + 1) * hpg, :])
        sums = t_upper_r if d == 0 else t_lower_r
        dtcs = [_softplus(dtt_ref[c][d * hpg:(d + 1) * hpg, :] + bias_t) for c in range(nc)]
        dta = jnp.concatenate(dtcs, axis=0) * jnp.concatenate([a_neg_t] * nc, axis=0)
        acs = _dot(jnp.concatenate(_split3(dta), axis=1), sums)
        for c in range(nc):
            dtct_scr[d, c] = dtcs[c]
            acst_scr[d, c] = acs[c * hpg:(c + 1) * hpg]
        if has_h0:
            s_scr[d] = jnp.transpose(h0_ref[d].reshape(hpg * hd, A_D_STATE))
        else:
            s_scr[d] = jnp.zeros((A_D_STATE, hpg * hd), F32)

    def spread(step, slot):
        for d in range(2):
            c = step if d == 0 else nc - 1 - step
            acs_p = pieces(acst_scr[d, c])
            dtw_scr[slot, d] = _dot_tn(pieces(dtct_scr[d, c]), expand_hd)
            acsw_scr[slot, d] = _dot_tn(acs_p, expand_hd)
            acsb_scr[slot, d] = _dot_tn(acs_p, expand_q)

    spread(0, 0)

    def step(ci, slot):
        ts = []
        for d in range(2):
            c = ci if d == 0 else nc - 1 - ci
            r0 = pl.multiple_of(c * q, q)
            ts.append(dict(d=d, rows=pl.ds(r0, q), acs_t=acst_scr[d, c], mask=lower if d == 0 else upper,
                           dt_w=dtw_scr[slot, d], acs_w=acsw_scr[slot, d], acs_b=acsb_scr[slot, d]))
        for t in ts:
            t["bb"] = b_ref[t["rows"], :].astype(BF16)
            t["cc"] = c_ref[t["rows"], :].astype(BF16)
            t["cb"] = _dot_nt(t["cc"], t["bb"])
            t["s_old"] = s_scr[t["d"]]
            t["cs"] = _dot(t["cc"], t["s_old"].astype(BF16))
            t["xdt"] = x_ref[t["rows"], :].astype(F32) * t["dt_w"]
            t["xdt_b"] = t["xdt"].astype(BF16)
            t["tot_w"] = t["acs_w"][q - 1:q, :] if t["d"] == 0 else t["acs_w"][0:1, :]
        spread(jnp.minimum(ci + 1, nc - 1), 1 - slot)
        for t in ts:
            t["y_diag"] = []
        for pr in range(hpg // 2):
            for t in ts:
                mms = []
                for r in (2 * pr, 2 * pr + 1):
                    seg = t["acs_b"][:, r * q:(r + 1) * q] - t["acs_t"][r:r + 1, :]
                    lmat = jnp.exp(jnp.where(t["mask"], seg, -jnp.inf))
                    mms.append((t["cb"] * lmat).astype(BF16))
                x_pair = t["xdt_b"][:, 2 * pr * hd:2 * (pr + 1) * hd]
                x_diag = jnp.concatenate([jnp.where(first_head, x_pair, jnp.zeros_like(x_pair)),
                                          jnp.where(first_head, jnp.zeros_like(x_pair), x_pair)], axis=0)
                t["y_diag"].append(_dot(jnp.concatenate(mms, axis=1), x_diag))
        for t in ts:
            y_off = jnp.exp(t["acs_w"]) * t["cs"]
            y_ref[t["rows"], :] += jnp.concatenate(t["y_diag"], axis=1) + y_off
            xs = (t["xdt"] * jnp.exp(t["tot_w"] - t["acs_w"])).astype(BF16)
            s_scr[t["d"]] = t["s_old"] * jnp.exp(t["tot_w"]) + _dot_tn(t["bb"], xs)

    def body(k, carry):
        step(2 * k, 0)
        step(2 * k + 1, 1)
        return carry

    lax.fori_loop(0, nc // 2, body, 0)
    if nc % 2:
        step(nc - 1, 0)
    if emit_state:
        for d in range(2):
            hfin_ref[d] = jnp.transpose(s_scr[d]).reshape(hpg, hd, A_D_STATE)

    y_ref[...] = y_ref[...] * _silu(z_ref[...].astype(F32))


def _ssd_core(proj, xbc, dt_row, params, layer, n_layers, h0, prev, states_prev, row0, n_seq, seq_len, emit_state):
    m = proj.shape[0]
    gd = A_GROUP_DIM
    ns = A_D_STATE
    hpg = A_HEADS_PER_GROUP
    sb = row0 // seq_len
    nc = seq_len // A_CHUNK
    x_blocks = A_D_INNER // gd
    bias_r, alog_r, dskip = params
    in_specs = [
        pl.BlockSpec((seq_len, gd), lambda n, g: (sb + n, g)),
        pl.BlockSpec((seq_len, gd), lambda n, g: (sb + n, g)),
        pl.BlockSpec((seq_len, ns), lambda n, g: (sb + n, x_blocks * (gd // ns) + g)),
        pl.BlockSpec((seq_len, ns), lambda n, g: (sb + n, x_blocks * (gd // ns) + A_N_GROUPS + g)),
        pl.BlockSpec((None, nc, 2 * hpg, A_CHUNK), lambda n, g: (g, sb + n, 0, 0)),
        pl.BlockSpec((None, 2 * hpg, 1), lambda n, g: (g, 0, 0)),
        pl.BlockSpec((None, 2 * hpg, 1), lambda n, g: (g, 0, 0)),
        pl.BlockSpec((None, 1, gd), lambda n, g: (g, 0, 0)),
    ]
    args = [xbc, proj, xbc, xbc, dt_row, bias_r, alog_r, dskip]
    aliases = {}
    state_spec = pl.BlockSpec((None, None, 2, hpg, A_HEAD_DIM, ns), lambda n, g: (n, layer, 0, g, 0, 0))
    if h0 is not None:
        in_specs.append(state_spec)
        args.append(h0)
    n_unread = 0
    for arr, out_idx in ((prev, 0), (states_prev, 1)):
        if arr is not None:
            in_specs.append(pl.BlockSpec(memory_space=pl.ANY))
            args.append(arr)
            aliases[len(args) - 1] = out_idx
            n_unread += 1
    out_specs = [pl.BlockSpec((seq_len, gd), lambda n, g: (sb + n, g))]
    out_shape = [jax.ShapeDtypeStruct((m, A_D_INNER), F32)]
    if emit_state:
        out_specs.append(state_spec)
        out_shape.append(jax.ShapeDtypeStruct((n_seq, n_layers, 2, A_N_HEADS, A_HEAD_DIM, ns), F32))
    res = pl.pallas_call(
        functools.partial(_ssd_kernel, seq_len=seq_len, has_h0=h0 is not None, emit_state=emit_state,
                          n_unread=n_unread),
        grid=(n_seq, A_N_GROUPS),
        in_specs=in_specs,
        out_specs=out_specs,
        out_shape=out_shape,
        scratch_shapes=[pltpu.VMEM((2, ns, gd), F32),
                        pltpu.VMEM((2, nc, hpg, A_CHUNK), F32),
                        pltpu.VMEM((2, nc, hpg, A_CHUNK), F32),
                        pltpu.VMEM((2, 2, A_CHUNK, gd), F32),
                        pltpu.VMEM((2, 2, A_CHUNK, gd), F32),
                        pltpu.VMEM((2, 2, A_CHUNK, hpg * A_CHUNK), F32)],
        input_output_aliases=aliases,
        compiler_params=_cparams(2),
        name="ssd_core",
    )(*args)
    return res if emit_state else (res[0], None)


def _ssd_mixer(y, mod, cond_of_tile, w_in, conv_w, conv_b, dt_bias, a_log, d_skip, layer, state0, states_prev,
               n_prompt, len_prompt, n_latent, len_latent):
    m = y.shape[0]
    mp = n_prompt * len_prompt
    proj = _mod_linear(y, mod, w_in, cond_of_tile)
    xbc = _conv_silu(proj, A_D_INNER, A_CONV_DIM, conv_w, conv_b, mp, len_prompt, len_latent)
    hpg = A_HEADS_PER_GROUP
    dt0 = A_D_INNER + A_CONV_DIM
    dt_raw = proj[:, dt0:dt0 + 2 * A_N_HEADS].astype(F32)
    dt_row = dt_raw.reshape(m // A_CHUNK, A_CHUNK, 2, A_N_GROUPS, hpg).transpose(3, 0, 2, 4, 1)
    dt_row = dt_row.reshape(A_N_GROUPS, m // A_CHUNK, 2 * hpg, A_CHUNK)

    def per_group(p):
        return p.reshape(2, A_N_GROUPS, hpg).transpose(1, 0, 2).reshape(A_N_GROUPS, 2 * hpg)

    bias_g = per_group(dt_bias.astype(F32))
    alog_g = per_group(a_log.astype(F32))
    dskip = jnp.repeat(d_skip.astype(F32), A_HEAD_DIM).reshape(A_N_GROUPS, 1, A_GROUP_DIM)
    params = (bias_g[:, :, None], alog_g[:, :, None], dskip)
    n_layers = state0.shape[1]
    out, st = _ssd_core(proj, xbc, dt_row, params, layer, n_layers, None, None, states_prev, 0, n_prompt,
                        len_prompt, True)
    out, _ = _ssd_core(proj, xbc, dt_row, params, layer, n_layers, state0.astype(F32), out, None, mp,
                       n_latent, len_latent, False)
    return out, st


def _gdn_kernel(*refs, seq_len, has_s0, emit_state):
    q_ref, k_ref, v_ref, z_ref, ab_ref, abt_ref, dtb_ref, alog_ref, ng_ref = refs[:9]
    pos = 9
    s0_ref = None
    if has_s0:
        s0_ref = refs[pos]
        pos += 2
    o_ref = refs[pos]
    pos += 1
    sfin_ref = None
    if emit_state:
        sfin_ref = refs[pos]
        pos += 1
    gq_scr, h_scr, et_scr, s_scr = refs[pos:pos + 4]

    hps = GDN_HEADS_PER_STEP
    hk, hv = B_HEAD_K, B_HEAD_V
    head0 = pl.program_id(1) * hps
    ch = B_CHUNK
    nc = seq_len // ch
    ii = lax.broadcasted_iota(jnp.int32, (ch, ch), 0)
    jj = lax.broadcasted_iota(jnp.int32, (ch, ch), 1)
    same_block = (ii // GDN_SOLVE_BLOCK) == (jj // GDN_SOLVE_BLOCK)
    dirs = []
    for d in range(2):
        dirs.append(dict(
            dt_bias=[dtb_ref[d, head0 + hh] for hh in range(hps)],
            a_neg=[-jnp.exp(alog_ref[d, head0 + hh]) for hh in range(hps)],
            incl=(ii >= jj) if d == 0 else (ii <= jj),
            incl_t=(ii <= jj) if d == 0 else (ii >= jj),
            strict=(ii > jj) if d == 0 else (ii < jj)))

    def mm(a, b):
        return _dot(a.astype(BF16), b.astype(BF16))

    def compose(a, b):
        return a + b + mm(a, b)

    n_prep = min(GDN_PREP_UNROLL, nc)

    def prepare(grp, carry):
        chains = []
        for j, hh in [(j, hh) for j in range(n_prep) for hh in range(hps)]:
            c = grp * n_prep + j
            rows = pl.ds(pl.multiple_of(c * ch, ch), ch)
            qc = q_ref[rows, hh * hk:(hh + 1) * hk].astype(F32)
            kc = k_ref[rows, hh * hk:(hh + 1) * hk].astype(F32)
            vc = v_ref[rows, hh * hv:(hh + 1) * hv].astype(F32)
            qn = qc * lax.rsqrt(jnp.sum(qc * qc, axis=-1, keepdims=True) + RMS_EPS) * (B_HEAD_K ** -0.5)
            kn = kc * lax.rsqrt(jnp.sum(kc * kc, axis=-1, keepdims=True) + RMS_EPS)
            knb = kn.astype(BF16)
            qk = _dot_nt(qn.astype(BF16), knb)
            kk = _dot_nt(knb, knb)
            ab = ab_ref[hh, rows, :]
            abt = abt_ref[hh, c]
            for d in range(2):
                p = dirs[d]
                a_neg, dt_bias = p["a_neg"][hh], p["dt_bias"][hh]
                beta = _sigmoid(ab[:, d:d + 1])
                g_col = a_neg * _softplus(ab[:, 2 + d:3 + d] + dt_bias)
                g_row = a_neg * _softplus(abt[2 + d:3 + d, :] + dt_bias)
                gcs = jnp.sum(jnp.where(p["incl"], jnp.broadcast_to(g_row, (ch, ch)), 0.0), axis=1, keepdims=True)
                gcs_t = jnp.sum(jnp.where(p["incl_t"], jnp.broadcast_to(g_col, (ch, ch)), 0.0), axis=0,
                                keepdims=True)
                tot = gcs[ch - 1:ch, :] if d == 0 else gcs[0:1, :]
                decay = jnp.exp(jnp.where(p["incl"], gcs - gcs_t, -jnp.inf))
                a_mat = jnp.where(p["strict"], beta * kk * decay, 0.0)
                a_diag = jnp.where(same_block, a_mat, 0.0)
                e_gcs = jnp.exp(gcs)
                chains.append(dict(
                    d=d, c=c, hh=hh, rows=rows, a_off=a_mat - a_diag, dx=-a_diag, pw=a_diag,
                    wu=jnp.concatenate([kn * (beta * e_gcs), vc * beta], axis=1),
                    kd=kn * jnp.exp(tot - gcs), qd=qn * e_gcs, qkm=(qk * decay).astype(BF16),
                    et=jnp.broadcast_to(jnp.exp(tot), (1, B_HEAD_V))))
        for _ in range(3):
            for t in chains:
                t["pw"] = mm(t["pw"], t["pw"])
            for t in chains:
                t["dx"] = compose(t["dx"], t["pw"])
        for t in chains:
            t["n"] = t["a_off"] + mm(t["dx"], t["a_off"])
        for t in chains:
            t["nn"] = mm(t["n"], t["n"])
        for t in chains:
            t["mx"] = compose(-t["n"], t["nn"])
        for t in chains:
            t["xb"] = compose(t["mx"], t["dx"]).astype(BF16)
        for t in chains:
            t["wu"] = (t["wu"] + _dot(t["xb"], t["wu"].astype(BF16))).astype(BF16)
        for t in chains:
            t["gh"] = _dot(jnp.transpose(t["kd"]).astype(BF16), t["wu"])
            t["qo"] = _dot(t["qkm"], t["wu"])
        for t in chains:
            d, c, hh = t["d"], t["c"], t["hh"]
            gq_scr[d, hh, c, 0:hk, :] = t["gh"][:, :hk].astype(BF16)
            gq_scr[d, hh, c, hk:, :] = (t["qd"] - t["qo"][:, :hk]).astype(BF16)
            h_scr[d, hh, c] = t["gh"][:, hk:]
            et_scr[d, hh, c] = t["et"]
        for fwd, bwd in zip(chains[0::2], chains[1::2]):
            hh = fwd["hh"]
            o_ref[fwd["rows"], hh * hv:(hh + 1) * hv] = fwd["qo"][:, hk:] + bwd["qo"][:, hk:]
        return carry

    lax.fori_loop(0, nc // n_prep, prepare, 0)

    for d in range(2):
        for hh in range(hps):
            if has_s0:
                s_scr[d, hh] = s0_ref[d, hh]
            else:
                s_scr[d, hh] = jnp.zeros((hk, hv), F32)

    def recur(ci, carry):
        olds = {}
        prods = {}
        for d in range(2):
            c = ci if d == 0 else nc - 1 - ci
            for hh in range(hps):
                olds[d, hh] = s_scr[d, hh]
                prods[d, hh] = _dot(gq_scr[d, hh, c], olds[d, hh].astype(BF16))
        for d in range(2):
            c = ci if d == 0 else nc - 1 - ci
            rows = pl.ds(pl.multiple_of(c * ch, ch), ch)
            for hh in range(hps):
                r = prods[d, hh]
                s_scr[d, hh] = olds[d, hh] * et_scr[d, hh, c] - r[:hk] + h_scr[d, hh, c]
                o_ref[rows, hh * hv:(hh + 1) * hv] += r[hk:]
        return carry

    lax.fori_loop(0, nc, recur, 0)
    if emit_state:
        for d in range(2):
            for hh in range(hps):
                sfin_ref[d, hh] = s_scr[d, hh]

    for hh in range(hps):
        cols = slice(hh * hv, (hh + 1) * hv)
        o = o_ref[:, cols]
        o = o * lax.rsqrt(jnp.mean(o * o, axis=-1, keepdims=True) + RMS_EPS) * ng_ref[...]
        o_ref[:, cols] = o * _silu(z_ref[:, cols].astype(F32))


def _gdn_core(proj, qkv, ab_col, ab_row, dt_bias, a_log, norm_g, s0, prev, row0, n_seq, seq_len, emit_state):
    m = proj.shape[0]
    hk, hv = B_HEAD_K, B_HEAD_V
    hps = GDN_HEADS_PER_STEP
    wk, wv = hps * hk, hps * hv
    sb = row0 // seq_len
    nc = seq_len // B_CHUNK
    in_specs = [
        pl.BlockSpec((seq_len, wk), lambda n, h: (sb + n, h)),
        pl.BlockSpec((seq_len, wk), lambda n, h: (sb + n, B_QK_DIM // wk + h)),
        pl.BlockSpec((seq_len, wv), lambda n, h: (sb + n, (2 * B_QK_DIM) // wv + h)),
        pl.BlockSpec((seq_len, wv), lambda n, h: (sb + n, B_CONV_DIM // wv + h)),
        pl.BlockSpec((hps, seq_len, 4), lambda n, h: (h, sb + n, 0)),
        pl.BlockSpec((hps, nc, 4, B_CHUNK), lambda n, h: (h, sb + n, 0, 0)),
        pl.BlockSpec(memory_space=pltpu.SMEM),
        pl.BlockSpec(memory_space=pltpu.SMEM),
        pl.BlockSpec((1, hv), lambda n, h: (0, 0)),
    ]
    args = [qkv, qkv, qkv, proj, ab_col, ab_row, dt_bias, a_log, norm_g.reshape(1, hv)]
    aliases = {}
    if s0 is not None:
        in_specs += [pl.BlockSpec((None, 2, hps, hk, hv), lambda n, h: (n, 0, h, 0, 0)),
                     pl.BlockSpec(memory_space=pl.ANY)]
        args += [s0, prev]
        aliases = {len(args) - 1: 0}
    out_specs = [pl.BlockSpec((seq_len, wv), lambda n, h: (sb + n, h))]
    out_shape = [jax.ShapeDtypeStruct((m, B_V_DIM), F32)]
    if emit_state:
        out_specs.append(pl.BlockSpec((None, 2, hps, hk, hv), lambda n, h: (n, 0, h, 0, 0)))
        out_shape.append(jax.ShapeDtypeStruct((n_seq, 2, B_N_HEADS, hk, hv), F32))
    res = pl.pallas_call(
        functools.partial(_gdn_kernel, seq_len=seq_len, has_s0=s0 is not None, emit_state=emit_state),
        grid=(n_seq, B_N_HEADS // hps),
        in_specs=in_specs,
        out_specs=out_specs,
        out_shape=out_shape,
        scratch_shapes=[
            pltpu.VMEM((2, hps, nc, hk + B_CHUNK, hk), BF16),
            pltpu.VMEM((2, hps, nc, hk, hv), F32),
            pltpu.VMEM((2, hps, nc, 1, hv), F32),
            pltpu.VMEM((2, hps, hk, hv), F32),
        ],
        input_output_aliases=aliases,
        compiler_params=_cparams(2),
        name="gdn_core",
    )(*args)
    return res if emit_state else (res[0], None)


def _gdn_mixer(y, mod, cond_of_tile, w_in, conv_w, conv_b, dt_bias, a_log, norm_g, state0, n_prompt, len_prompt,
               n_latent, len_latent):
    m = y.shape[0]
    mp = n_prompt * len_prompt
    proj = _mod_linear(y, mod, w_in, cond_of_tile)
    qkv = _conv_silu(proj, 0, B_CONV_DIM, conv_w, conv_b, mp, len_prompt, len_latent)
    ab0 = B_CONV_DIM + B_V_DIM
    ab = proj[:, ab0:ab0 + 4 * B_N_HEADS].astype(F32).reshape(m, 4, B_N_HEADS)
    ab_col = ab.transpose(2, 0, 1)
    ab_row = ab.reshape(m // B_CHUNK, B_CHUNK, 4, B_N_HEADS).transpose(3, 0, 2, 1)
    dt_bias = dt_bias.astype(F32)
    a_log = a_log.astype(F32)
    out, st = _gdn_core(proj, qkv, ab_col, ab_row, dt_bias, a_log, norm_g, None, None, 0, n_prompt, len_prompt, True)
    out, _ = _gdn_core(proj, qkv, ab_col, ab_row, dt_bias, a_log, norm_g, state0, out, mp, n_latent, len_latent,
                       False)
    return out, st


KV_PAIR = 2
PAIR_Q = KV_PAIR * C_GROUP * C_HEAD_DIM
PAIR_KV = KV_PAIR * C_HEAD_DIM


def _softmax_pv(heads, values):
    mxs = []
    for scores, sink in heads:
        mx = sink
        for s in scores:
            mx = jnp.maximum(mx, jnp.max(s, axis=-1, keepdims=True))
        mxs.append(mx)
    probs = [[jnp.exp(s - mx) for s in scores] for (scores, _), mx in zip(heads, mxs)]
    dens = []
    for (_, sink), mx, ps in zip(heads, mxs, probs):
        den = jnp.exp(sink - mx)
        for p in ps:
            den = den + jnp.sum(p, axis=-1, keepdims=True)
        dens.append(den)
    outs = []
    for ps in probs:
        acc = None
        for p, v in zip(ps, values):
            pv = _dot(p.astype(BF16), v)
            acc = pv if acc is None else acc + pv
        outs.append(acc)
    return [acc / den for acc, den in zip(outs, dens)]


def _attn_ctx_kernel(q_ref, k_ref, v_ref, sink_ref, o_ref):
    pair = pl.program_id(1)
    hd = C_HEAD_DIM
    for gg in range(KV_PAIR):
        kb = k_ref[:, gg * hd:(gg + 1) * hd].astype(BF16)
        vb = v_ref[:, gg * hd:(gg + 1) * hd].astype(BF16)
        heads = []
        for r in range(C_GROUP):
            col = (gg * C_GROUP + r) * hd
            qb = q_ref[:, col:col + hd].astype(BF16)
            sink = sink_ref[pair * (KV_PAIR * C_GROUP) + gg * C_GROUP + r]
            heads.append(([_dot_nt(qb, kb) * C_SCALE], sink))
        for r, o in enumerate(_softmax_pv(heads, [vb])):
            col = (gg * C_GROUP + r) * hd
            o_ref[:, col:col + hd] = o


def _attn_ctx(proj, sink, n_prompt, len_prompt):
    m = proj.shape[0]
    k_block0 = C_Q_DIM // PAIR_KV
    v_block0 = (C_Q_DIM + C_KV_DIM) // PAIR_KV
    return pl.pallas_call(
        _attn_ctx_kernel,
        grid=(n_prompt, C_N_KV // KV_PAIR),
        in_specs=[
            pl.BlockSpec((len_prompt, PAIR_Q), lambda n, p: (n, p)),
            pl.BlockSpec((len_prompt, PAIR_KV), lambda n, p: (n, k_block0 + p)),
            pl.BlockSpec((len_prompt, PAIR_KV), lambda n, p: (n, v_block0 + p)),
            pl.BlockSpec(memory_space=pltpu.SMEM),
        ],
        out_specs=pl.BlockSpec((len_prompt, PAIR_Q), lambda n, p: (n, p)),
        out_shape=jax.ShapeDtypeStruct((m, C_Q_DIM), F32),
        compiler_params=_cparams(2),
        name="attn_ctx",
    )(proj, proj, proj, sink)


def _rope(x, cos, sin):
    width = x.shape[1]
    lane = lax.broadcasted_iota(jnp.int32, x.shape, 1)
    nf = C_HEAD_DIM // 4
    partner = jnp.where(lane % (2 * nf) < nf, pltpu.roll(x, width - nf, 1), pltpu.roll(x, nf, 1))
    return x * cos + partner * sin


def _attn_lat_kernel(q_ref, kp_ref, vp_ref, kc_ref, vc_ref, cq_ref, sq_ref, ck_ref, sk_ref, sink_ref, prev_ref, o_ref,
                     *, seq_len):
    del prev_ref
    pair = pl.program_id(1)
    qi = pl.program_id(2)
    hd = C_HEAD_DIM
    blk = C_BLOCK
    span = C_BLOCK + 2 * C_WINDOW
    start = pl.multiple_of(qi * blk, blk)
    q = _rope(q_ref[...].astype(F32), cq_ref[...], sq_ref[...])
    kw = _rope(kp_ref[pl.ds(start, span), :].astype(F32), ck_ref[pl.ds(start, span), :],
               sk_ref[pl.ds(start, span), :])
    vw = vp_ref[pl.ds(start, span), :]
    iq = lax.broadcasted_iota(jnp.int32, (blk, span), 0)
    jk = lax.broadcasted_iota(jnp.int32, (blk, span), 1)
    kpos = start - C_WINDOW + jk
    ok = jnp.logical_and(jnp.abs(iq + C_WINDOW - jk) <= C_WINDOW, jnp.logical_and(kpos >= 0, kpos < seq_len))
    for gg in range(KV_PAIR):
        kb = kw[:, gg * hd:(gg + 1) * hd].astype(BF16)
        vb = vw[:, gg * hd:(gg + 1) * hd].astype(BF16)
        kcb = kc_ref[:, gg * hd:(gg + 1) * hd].astype(BF16)
        vcb = vc_ref[:, gg * hd:(gg + 1) * hd].astype(BF16)
        heads = []
        for r in range(C_GROUP):
            col = (gg * C_GROUP + r) * hd
            qb = q[:, col:col + hd].astype(BF16)
            s_loc = jnp.where(ok, _dot_nt(qb, kb) * C_SCALE, -jnp.inf)
            s_ctx = _dot_nt(qb, kcb) * C_SCALE
            sink = sink_ref[pair * (KV_PAIR * C_GROUP) + gg * C_GROUP + r]
            heads.append(([s_loc, s_ctx], sink))
        for r, o in enumerate(_softmax_pv(heads, [vb, vcb])):
            col = (gg * C_GROUP + r) * hd
            o_ref[:, col:col + hd] = o


def _rope_tables(seq_len):
    nf = C_HEAD_DIM // 4
    t = jnp.arange(seq_len)
    inv_freq = ROPE_BASE ** (-jnp.arange(nf, dtype=F32) / nf)
    ang_r = (t // GRID_W).astype(F32)[:, None] * inv_freq
    ang_c = (t % GRID_W).astype(F32)[:, None] * inv_freq
    cos = jnp.concatenate([jnp.cos(ang_r), jnp.cos(ang_r), jnp.cos(ang_c), jnp.cos(ang_c)], axis=1)
    sin = jnp.concatenate([-jnp.sin(ang_r), jnp.sin(ang_r), -jnp.sin(ang_c), jnp.sin(ang_c)], axis=1)
    return cos, sin


def _attn_lat(proj, cache_k, cache_v, sink, prev, row0, n_latent, len_latent):
    m = proj.shape[0]
    past = cache_k.shape[1]
    w = C_WINDOW
    k_lat = proj[row0:, C_Q_DIM:C_Q_DIM + C_KV_DIM].reshape(n_latent, len_latent, C_KV_DIM)
    v_lat = proj[row0:, C_Q_DIM + C_KV_DIM:C_Q_DIM + 2 * C_KV_DIM].reshape(n_latent, len_latent, C_KV_DIM)
    kp = jnp.pad(k_lat, ((0, 0), (w, w), (0, 0)))
    vp = jnp.pad(v_lat, ((0, 0), (w, w), (0, 0)))
    kc = cache_k.reshape(n_latent, past, C_KV_DIM).astype(F32)
    vc = cache_v.reshape(n_latent, past, C_KV_DIM).astype(F32)
    cos, sin = _rope_tables(len_latent)
    cos_q = jnp.tile(cos, (1, PAIR_Q // C_HEAD_DIM))
    sin_q = jnp.tile(sin, (1, PAIR_Q // C_HEAD_DIM))
    cos_k = jnp.pad(jnp.tile(cos, (1, KV_PAIR)), ((w, w), (0, 0)))
    sin_k = jnp.pad(jnp.tile(sin, (1, KV_PAIR)), ((w, w), (0, 0)))
    qb0 = row0 // C_BLOCK
    nqb = len_latent // C_BLOCK
    padded = len_latent + 2 * w
    return pl.pallas_call(
        functools.partial(_attn_lat_kernel, seq_len=len_latent),
        grid=(n_latent, C_N_KV // KV_PAIR, nqb),
        in_specs=[
            pl.BlockSpec((C_BLOCK, PAIR_Q), lambda n, p, i: (qb0 + n * nqb + i, p)),
            pl.BlockSpec((None, padded, PAIR_KV), lambda n, p, i: (n, 0, p)),
            pl.BlockSpec((None, padded, PAIR_KV), lambda n, p, i: (n, 0, p)),
            pl.BlockSpec((None, past, PAIR_KV), lambda n, p, i: (n, 0, p)),
            pl.BlockSpec((None, past, PAIR_KV), lambda n, p, i: (n, 0, p)),
            pl.BlockSpec((C_BLOCK, PAIR_Q), lambda n, p, i: (i, 0)),
            pl.BlockSpec((C_BLOCK, PAIR_Q), lambda n, p, i: (i, 0)),
            pl.BlockSpec((padded, PAIR_KV), lambda n, p, i: (0, 0)),
            pl.BlockSpec((padded, PAIR_KV), lambda n, p, i: (0, 0)),
            pl.BlockSpec(memory_space=pltpu.SMEM),
            pl.BlockSpec(memory_space=pl.ANY),
        ],
        out_specs=pl.BlockSpec((C_BLOCK, PAIR_Q), lambda n, p, i: (qb0 + n * nqb + i, p)),
        out_shape=jax.ShapeDtypeStruct((m, C_Q_DIM), F32),
        input_output_aliases={10: 0},
        compiler_params=_cparams(3),
        name="attn_lat",
    )(proj, kp, vp, kc, vc, cos_q, sin_q, cos_k, sin_k, sink, prev)


def _attn_mixer(y, mod, cond_of_tile, w_in, sink, cache_k, cache_v, n_prompt, len_prompt, n_latent, len_latent):
    m = y.shape[0]
    mp = n_prompt * len_prompt
    proj = _mod_linear(y, mod, w_in, cond_of_tile)
    sink = sink.astype(F32)
    out = _attn_ctx(proj, sink, n_prompt, len_prompt)
    out = _attn_lat(proj, cache_k, cache_v, sink, out, mp, n_latent, len_latent)
    new_k = proj[:mp, C_Q_DIM:C_Q_DIM + C_KV_DIM].astype(F32).reshape(n_prompt, len_prompt, C_N_KV, C_HEAD_DIM)
    new_v = proj[:mp, C_Q_DIM + C_KV_DIM:C_Q_DIM + 2 * C_KV_DIM].astype(F32)
    new_v = new_v.reshape(n_prompt, len_prompt, C_N_KV, C_HEAD_DIM)
    return out, new_k, new_v


def kernel(x_prompt, x_sample, state_ssd, state_delta, cache_k, cache_v, c, c_ctx, w_mod, b_mod, ln_g, ln_b, ffn_w_gate, ffn_w_up, ffn_w_down, ssd_w_in, ssd_conv_w, ssd_conv_b, ssd_dt_bias, ssd_a_log, ssd_d, ssd_norm, ssd_w_out, gdn_w_in, gdn_conv_w, gdn_conv_b, gdn_dt_bias, gdn_a_log, gdn_norm, gdn_w_out, attn_w_in, attn_sink, attn_w_out):
    n_prompt, len_prompt, d = x_prompt.shape
    n_latent, len_latent, _ = x_sample.shape
    mp = n_prompt * len_prompt
    assert d == D_MODEL and n_latent + 1 <= N_COND
    assert len_prompt % ROW_TILE == 0 or ROW_TILE % len_prompt == 0
    assert mp % ROW_TILE == 0 and len_latent % ROW_TILE == 0
    assert mp % len_latent == 0 and len_latent % len_prompt == 0 and len_latent % GRID_W == 0

    assert mp % FFN_ROW_TILE == 0 and len_latent % FFN_ROW_TILE == 0

    def cond_of_tile(i, tile_rows=ROW_TILE):
        return _cond_index(i, mp // tile_rows, len_latent // tile_rows)

    cond = jnp.concatenate([c_ctx[None].astype(F32), c.astype(F32),
                            jnp.zeros((N_COND - 1 - n_latent, d), F32)], axis=0)
    mods = _adaln(cond, w_mod, b_mod).reshape(DEPTH, N_COND, N_MOD, d)

    y = (x_prompt.reshape(mp, d).astype(F32), x_sample.reshape(n_latent * len_latent, d).astype(F32))
    w_gate, w_up, w_down = ffn_w_gate.astype(BF16), ffn_w_up.astype(BF16), ffn_w_down.astype(BF16)
    seqs = (n_prompt, len_prompt, n_latent, len_latent)
    ssd_states, gdn_states, k_list, v_list = None, [], [], []
    for i in range(DEPTH):
        mod = mods[i]
        y = _ffn_half(y, mod, 0, ln_g[i, 0], ln_b[i, 0], w_gate, w_up, w_down, i, 0, cond_of_tile, mp)
        kind, j = i % 3, i // 3
        if kind == 0:
            a, ssd_states = _ssd_mixer(y, mod, cond_of_tile, ssd_w_in[j], ssd_conv_w[j], ssd_conv_b[j],
                                       ssd_dt_bias[j], ssd_a_log[j], ssd_d[j], j, state_ssd, ssd_states, *seqs)
            y = _out_proj_ln(a, y, mod, ssd_w_out[j], ln_g[i, 1], ln_b[i, 1], cond_of_tile, rms_g=ssd_norm[j])
        elif kind == 1:
            a, st = _gdn_mixer(y, mod, cond_of_tile, gdn_w_in[j], gdn_conv_w[j], gdn_conv_b[j], gdn_dt_bias[j],
                               gdn_a_log[j], gdn_norm[j], state_delta[:, j], *seqs)
            gdn_states.append(st)
            y = _out_proj_ln(a, y, mod, gdn_w_out[j], ln_g[i, 1], ln_b[i, 1], cond_of_tile)
        else:
            a, kc, vc = _attn_mixer(y, mod, cond_of_tile, attn_w_in[j], attn_sink[j], cache_k[:, j], cache_v[:, j],
                                    *seqs)
            k_list.append(kc)
            v_list.append(vc)
            y = _out_proj_ln(a, y, mod, attn_w_out[j], ln_g[i, 1], ln_b[i, 1], cond_of_tile)
        y = _ffn_half(y, mod, 2, ln_g[i, 2], ln_b[i, 2], w_gate, w_up, w_down, i, 1, cond_of_tile, mp,
                      split_out=i == DEPTH - 1)

    y_prompt = y[0].reshape(n_prompt, len_prompt, d)
    y_sample = y[1].reshape(n_latent, len_latent, d)
    return (y_prompt, y_sample, ssd_states, jnp.stack(gdn_states, axis=1),
            jnp.stack(k_list, axis=1), jnp.stack(v_list, axis=1))
```

```python
import functools
import math

import jax
import jax.numpy as jnp
from jax import lax
from jax.experimental import pallas as pl
from jax.experimental.pallas import tpu as pltpu

F32 = jnp.float32
BF16 = jnp.bfloat16

D_MODEL = 1024
DEPTH = 4
GRID_W = 64
N_MOD = 9
D_FF = 2816
DEEPNORM_ALPHA = (2 * DEPTH) ** 0.25
LN_EPS = 1e-5
RMS_EPS = 1e-6
FFN_RES = 0.5

A_D_INNER = 2 * D_MODEL
A_HEAD_DIM = 64
A_N_HEADS = A_D_INNER // A_HEAD_DIM
A_N_GROUPS = 4
A_HEADS_PER_GROUP = A_N_HEADS // A_N_GROUPS
A_GROUP_DIM = A_HEADS_PER_GROUP * A_HEAD_DIM
A_D_STATE = 128
A_CHUNK = 128
A_CONV_DIM = A_D_INNER + 2 * A_N_GROUPS * A_D_STATE

B_N_HEADS = 8
B_HEAD_K = 128
B_HEAD_V = 256
B_CHUNK = 64
B_QK_DIM = B_N_HEADS * B_HEAD_K
B_V_DIM = B_N_HEADS * B_HEAD_V
B_CONV_DIM = 2 * B_QK_DIM + B_V_DIM
GDN_SOLVE_BLOCK = 16
GDN_PREP_UNROLL = 4
GDN_HEADS_PER_STEP = 2

C_N_HEADS = 16
C_N_KV = 4
C_GROUP = C_N_HEADS // C_N_KV
C_HEAD_DIM = 64
C_WINDOW = 128
C_BLOCK = 128
C_Q_DIM = C_N_HEADS * C_HEAD_DIM
C_KV_DIM = C_N_KV * C_HEAD_DIM
C_SCALE = C_HEAD_DIM ** -0.5
ROPE_BASE = 10000.0

N_COND = 16
ROW_TILE = 512
PROJ_TILE_MAX = 3328
PROJ_ALIGN = 256
FF_TILE = 1408
FFN_ROW_TILE = 512
FFN_SUBTILES = 2
CONV_KERNEL = 5
CONV_PAD = 16
CONV_COLS = 256
CONV_HALO = 8
PROJ_CONV_SUBTILES = 4
VMEM_LIMIT = 56 * 1024 * 1024


def _cparams(n_axes):
    return pltpu.CompilerParams(dimension_semantics=("arbitrary",) * n_axes, vmem_limit_bytes=VMEM_LIMIT)


def _dot(a, b, precision=None):
    return lax.dot_general(a, b, (((1,), (0,)), ((), ())), precision=precision, preferred_element_type=F32)


def _dot_nt(a, b, precision=None):
    return lax.dot_general(a, b, (((1,), (1,)), ((), ())), precision=precision, preferred_element_type=F32)


def _dot_tn(a, b, precision=None):
    return lax.dot_general(a, b, (((0,), (0,)), ((), ())), precision=precision, preferred_element_type=F32)


def _split3(x):
    hi = x.astype(BF16)
    rest = x - hi.astype(F32)
    mid = rest.astype(BF16)
    lo = (rest - mid.astype(F32)).astype(BF16)
    return [hi, mid, lo]


def _sigmoid(x):
    return 1.0 / (1.0 + jnp.exp(-x))


def _silu(x):
    return x * _sigmoid(x)


def _softplus(x):
    return jnp.maximum(x, 0.0) + jnp.log(1.0 + jnp.exp(-jnp.abs(x)))


def _layer_norm(t, g, b):
    mu = jnp.mean(t, axis=-1, keepdims=True)
    tc = t - mu
    var = jnp.mean(tc * tc, axis=-1, keepdims=True)
    return tc * lax.rsqrt(var + LN_EPS) * g + b


def _cond_index(i, n_prompt_tiles, tiles_per_request):
    return jnp.where(i < n_prompt_tiles, 0, 1 + (jnp.maximum(i - n_prompt_tiles, 0)) // tiles_per_request)


def _adaln_kernel(c_ref, w_ref, b_ref, o_ref):
    c = c_ref[...]
    h = _silu(c).astype(BF16)
    o_ref[...] = _dot(h, w_ref[...].astype(BF16)) + b_ref[...]


def _adaln(cond, w_mod, b_mod):
    n_out = N_MOD * D_MODEL
    tn = D_MODEL
    return pl.pallas_call(
        _adaln_kernel,
        grid=(DEPTH, n_out // tn),
        in_specs=[
            pl.BlockSpec((N_COND, D_MODEL), lambda l, j: (0, 0)),
            pl.BlockSpec((None, D_MODEL, tn), lambda l, j: (l, 0, j)),
            pl.BlockSpec((None, 1, tn), lambda l, j: (l, 0, j)),
        ],
        out_specs=pl.BlockSpec((None, N_COND, tn), lambda l, j: (l, 0, j)),
        out_shape=jax.ShapeDtypeStruct((DEPTH, N_COND, n_out), F32),
        compiler_params=_cparams(2),
        name="adaln",
    )(cond, w_mod, b_mod.reshape(DEPTH, 1, n_out))


def _ffn_kernel(*refs, s, n_in, n_out, n_prompt_tiles):
    y_refs = refs[:n_in]
    mod_ref, g_ref, b_ref, wg_ref, wu_ref, wd_ref = refs[n_in:n_in + 6]
    o_refs = refs[n_in + 6:]
    is_prompt = pl.program_id(0) < n_prompt_tiles
    shift = mod_ref[pl.ds(3 * s, 1), :]
    scale = mod_ref[pl.ds(3 * s + 1, 1), :]
    gate = mod_ref[pl.ds(3 * s + 2, 1), :]
    sub = FFN_ROW_TILE // FFN_SUBTILES
    for r in range(FFN_SUBTILES):
        rows = slice(r * sub, (r + 1) * sub)
        if n_in == 1:
            y = y_refs[0][rows, :]
        else:
            y = jnp.where(is_prompt, y_refs[0][rows, :], y_refs[1][rows, :])
        h = (y * (1.0 + scale) + shift).astype(BF16)
        acc = None
        for c in range(D_FF // FF_TILE):
            cols = slice(c * FF_TILE, (c + 1) * FF_TILE)
            a = _dot(h, wg_ref[:, cols])
            u = _dot(h, wu_ref[:, cols])
            f = (_silu(a) * u).astype(BF16)
            part = _dot(f, wd_ref[cols, :])
            acc = part if acc is None else acc + part
        t = DEEPNORM_ALPHA * y + (FFN_RES * gate) * acc
        res = _layer_norm(t, g_ref[...], b_ref[...])
        if n_out == 1:
            o_refs[0][rows, :] = res
        else:
            @pl.when(is_prompt)
            def _(res=res, rows=rows):
                o_refs[0][rows, :] = res

            @pl.when(jnp.logical_not(is_prompt))
            def _(res=res, rows=rows):
                o_refs[1][rows, :] = res


def _ffn_half(ys, mod, s, g, b, w_gate, w_up, w_down, layer, half, cond_of_tile, n_prompt_rows, split_out=False):
    npt = n_prompt_rows // FFN_ROW_TILE
    split_in = isinstance(ys, (tuple, list))
    ys = tuple(ys) if split_in else (ys,)
    m = sum(a.shape[0] for a in ys)
    resident = pl.Buffered(1)

    def whole(i):
        return (i, 0)

    def prompt_part(i):
        return (jnp.minimum(i, npt - 1), 0)

    def latent_part(i):
        return (jnp.maximum(i - npt, 0), 0)

    def row_specs(split):
        return [pl.BlockSpec((FFN_ROW_TILE, D_MODEL), f)
                for f in ((prompt_part, latent_part) if split else (whole,))]

    if split_out:
        out_shape = [jax.ShapeDtypeStruct((n_prompt_rows, D_MODEL), F32),
                     jax.ShapeDtypeStruct((m - n_prompt_rows, D_MODEL), F32)]
    else:
        out_shape = [jax.ShapeDtypeStruct((m, D_MODEL), F32)]
    res = pl.pallas_call(
        functools.partial(_ffn_kernel, s=s, n_in=len(ys), n_out=len(out_shape), n_prompt_tiles=npt),
        grid=(m // FFN_ROW_TILE,),
        in_specs=row_specs(split_in) + [
            pl.BlockSpec((None, N_MOD, D_MODEL), lambda i: (cond_of_tile(i, FFN_ROW_TILE), 0, 0)),
            pl.BlockSpec((1, D_MODEL), lambda i: (0, 0)),
            pl.BlockSpec((1, D_MODEL), lambda i: (0, 0)),
            pl.BlockSpec((None, None, D_MODEL, D_FF), lambda i: (layer, half, 0, 0), pipeline_mode=resident),
            pl.BlockSpec((None, None, D_MODEL, D_FF), lambda i: (layer, half, 0, 0), pipeline_mode=resident),
            pl.BlockSpec((None, None, D_FF, D_MODEL), lambda i: (layer, half, 0, 0), pipeline_mode=resident),
        ],
        out_specs=row_specs(split_out),
        out_shape=out_shape,
        compiler_params=_cparams(1),
        name="ffn_half",
    )(*ys, mod, g.reshape(1, -1), b.reshape(1, -1), w_gate, w_up, w_down)
    return tuple(res) if split_out else res[0]


def _modlin_kernel(y_ref, mod_ref, w_ref, o_ref):
    shift = mod_ref[pl.ds(3, 1), :]
    scale = mod_ref[pl.ds(4, 1), :]
    h = (y_ref[...] * (1.0 + scale) + shift).astype(BF16)
    o_ref[...] = _dot(h, w_ref[...]).astype(o_ref.dtype)


def _mod_linear(y, mod, w, cond_of_tile):
    m = y.shape[0]
    n = w.shape[1]
    n_tiles = -(-n // PROJ_TILE_MAX)
    tn = -(-n // (n_tiles * PROJ_ALIGN)) * PROJ_ALIGN
    n_pad = n_tiles * tn
    w = jnp.pad(w.astype(BF16), ((0, 0), (0, n_pad - n)))
    return pl.pallas_call(
        _modlin_kernel,
        grid=(n_tiles, m // ROW_TILE),
        in_specs=[
            pl.BlockSpec((ROW_TILE, D_MODEL), lambda j, i: (i, 0)),
            pl.BlockSpec((None, N_MOD, D_MODEL), lambda j, i: (cond_of_tile(i), 0, 0)),
            pl.BlockSpec((D_MODEL, tn), lambda j, i: (0, j)),
        ],
        out_specs=pl.BlockSpec((ROW_TILE, tn), lambda j, i: (i, j)),
        out_shape=jax.ShapeDtypeStruct((m, n_pad), BF16),
        compiler_params=_cparams(2),
        name="mod_linear",
    )(y, mod, w)


def _modlin_conv_kernel(yp_ref, y_ref, yn_ref, mod_ref, w_ref, cw_ref, cb_ref, o_ref, h_scr, *acc_scrs,
                        n_prompt_rows, len_prompt, len_latent, conv_ranges):
    j = pl.program_id(0)
    i = pl.program_id(1)
    halo = CONV_HALO
    half = CONV_KERNEL // 2
    tn = o_ref.shape[1]
    n_sub = len(acc_scrs)
    sub = ROW_TILE // n_sub
    row0 = i * ROW_TILE
    is_prompt = row0 < n_prompt_rows
    pos = lax.rem(jnp.maximum(row0 - n_prompt_rows, 0), len_latent)
    shift = mod_ref[pl.ds(3, 1), :]
    scale = mod_ref[pl.ds(4, 1), :]
    h_scr[0:halo, :] = yp_ref[...] * (1.0 + scale) + shift
    h_scr[halo:halo + ROW_TILE, :] = y_ref[...] * (1.0 + scale) + shift
    h_scr[halo + ROW_TILE:, :] = yn_ref[...] * (1.0 + scale) + shift

    def project(s):
        acc = _dot(h_scr[s * sub:(s + 1) * sub + 2 * halo, :].astype(BF16), w_ref[...])
        first = jnp.where(is_prompt, (s * sub) % len_prompt == 0, jnp.logical_and(s == 0, pos == 0))
        last = jnp.where(is_prompt, ((s + 1) * sub) % len_prompt == 0,
                         jnp.logical_and(s == n_sub - 1, pos + ROW_TILE == len_latent))
        acc_scrs[s][halo:halo + sub, :] = acc[halo:halo + sub]
        acc_scrs[s][0:halo, :] = acc[0:halo] * jnp.where(first, 0.0, 1.0)
        acc_scrs[s][halo + sub:, :] = acc[halo + sub:] * jnp.where(last, 0.0, 1.0)

    def finish(s, c0, c1):
        rows = slice(s * sub, (s + 1) * sub)
        for a, b in ((0, c0), (c1, tn)):
            if b > a:
                o_ref[rows, a:b] = acc_scrs[s][halo:halo + sub, a:b].astype(o_ref.dtype)
        for c in range(c0, c1, CONV_COLS):
            cols = slice(c, min(c + CONV_COLS, c1))
            acc = cb_ref[:, cols]
            for k in range(CONV_KERNEL):
                acc = acc + cw_ref[pl.ds(k, 1), cols] * acc_scrs[s][pl.ds(halo + k - half, sub), cols]
            o_ref[rows, cols] = _silu(acc).astype(o_ref.dtype)

    def body(c0, c1):
        project(0)
        for s in range(n_sub):
            if s + 1 < n_sub:
                project(s + 1)
            finish(s, c0, c1)

    for jj, (c0, c1) in enumerate(conv_ranges):
        pl.when(j == jj)(functools.partial(body, c0, c1))


def _mod_linear_conv(y, mod, w, conv_w, conv_b, col0, cond_of_tile, n_prompt_rows, len_prompt, len_latent):
    m = y.shape[0]
    n = w.shape[1]
    width = conv_w.shape[1]
    n_tiles = -(-n // PROJ_TILE_MAX)
    tn = -(-n // (n_tiles * PROJ_ALIGN)) * PROJ_ALIGN
    n_pad = n_tiles * tn
    assert ROW_TILE % len_prompt == 0 and len_latent % ROW_TILE == 0 and n_prompt_rows % ROW_TILE == 0
    assert len_prompt % (ROW_TILE // PROJ_CONV_SUBTILES) == 0
    assert col0 % 128 == 0 and width % 128 == 0 and tn % 128 == 0
    w = jnp.pad(w.astype(BF16), ((0, 0), (0, n_pad - n)))
    cw = jnp.pad(conv_w.astype(F32), ((0, 0), (col0, n_pad - col0 - width)))
    cb = jnp.pad(conv_b.astype(F32).reshape(1, -1), ((0, 0), (col0, n_pad - col0 - width)))
    conv_ranges = []
    for jj in range(n_tiles):
        c0 = min(max(col0 - jj * tn, 0), tn)
        c1 = min(max(col0 + width - jj * tn, 0), tn)
        conv_ranges.append((c0, max(c0, c1)))
    halo = CONV_HALO
    per_tile = ROW_TILE // halo
    return pl.pallas_call(
        functools.partial(_modlin_conv_kernel, n_prompt_rows=n_prompt_rows, len_prompt=len_prompt,
                          len_latent=len_latent, conv_ranges=tuple(conv_ranges)),
        grid=(n_tiles, m // ROW_TILE),
        in_specs=[
            pl.BlockSpec((halo, D_MODEL), lambda j, i: (jnp.maximum(i * per_tile - 1, 0), 0)),
            pl.BlockSpec((ROW_TILE, D_MODEL), lambda j, i: (i, 0)),
            pl.BlockSpec((halo, D_MODEL), lambda j, i: (jnp.minimum((i + 1) * per_tile, m // halo - 1), 0)),
            pl.BlockSpec((None, N_MOD, D_MODEL), lambda j, i: (cond_of_tile(i), 0, 0)),
            pl.BlockSpec((D_MODEL, tn), lambda j, i: (0, j)),
            pl.BlockSpec((CONV_KERNEL, tn), lambda j, i: (0, j)),
            pl.BlockSpec((1, tn), lambda j, i: (0, j)),
        ],
        out_specs=pl.BlockSpec((ROW_TILE, tn), lambda j, i: (i, j)),
        out_shape=jax.ShapeDtypeStruct((m, n_pad), BF16),
        scratch_shapes=[pltpu.VMEM((ROW_TILE + 2 * halo, D_MODEL), F32)]
        + [pltpu.VMEM((ROW_TILE // PROJ_CONV_SUBTILES + 2 * halo, tn), F32)] * PROJ_CONV_SUBTILES,
        compiler_params=_cparams(2),
        name="mod_linear_conv",
    )(y, y, y, mod, w, cw, cb)


def _outproj_kernel(a_ref, y_ref, mod_ref, w_ref, g_ref, b_ref, *rest, rms):
    if rms:
        ng_ref, o_ref = rest
    else:
        (o_ref,) = rest
    a = a_ref[...]
    if rms:
        a = a * lax.rsqrt(jnp.mean(a * a, axis=-1, keepdims=True) + RMS_EPS) * ng_ref[...]
    mix = _dot(a.astype(BF16), w_ref[...])
    gate = mod_ref[pl.ds(5, 1), :]
    t = DEEPNORM_ALPHA * y_ref[...] + gate * mix
    o_ref[...] = _layer_norm(t, g_ref[...], b_ref[...])


def _out_proj_ln(a, y, mod, w, g, b, cond_of_tile, rms_g=None):
    m, k = a.shape
    in_specs = [
        pl.BlockSpec((ROW_TILE, k), lambda i: (i, 0)),
        pl.BlockSpec((ROW_TILE, D_MODEL), lambda i: (i, 0)),
        pl.BlockSpec((None, N_MOD, D_MODEL), lambda i: (cond_of_tile(i), 0, 0)),
        pl.BlockSpec((k, D_MODEL), lambda i: (0, 0)),
        pl.BlockSpec((1, D_MODEL), lambda i: (0, 0)),
        pl.BlockSpec((1, D_MODEL), lambda i: (0, 0)),
    ]
    args = [a, y, mod, w.astype(BF16), g.reshape(1, -1), b.reshape(1, -1)]
    if rms_g is not None:
        in_specs.append(pl.BlockSpec((1, k), lambda i: (0, 0)))
        args.append(rms_g.reshape(1, -1))
    return pl.pallas_call(
        functools.partial(_outproj_kernel, rms=rms_g is not None),
        grid=(m // ROW_TILE,),
        in_specs=in_specs,
        out_specs=pl.BlockSpec((ROW_TILE, D_MODEL), lambda i: (i, 0)),
        out_shape=jax.ShapeDtypeStruct((m, D_MODEL), F32),
        compiler_params=_cparams(1),
        name="out_proj_ln",
    )(*args)


def _conv_kernel(x_ref, w_ref, b_ref, o_ref, xs_scr, *, n_prompt_tiles, len_prompt):
    rows = x_ref.shape[0]
    pad = CONV_PAD
    half = CONV_KERNEL // 2
    zeros = jnp.zeros((pad, xs_scr.shape[1]), F32)
    xs_scr[0:pad, :] = zeros
    xs_scr[pad + rows:, :] = zeros
    xs_scr[pad:pad + rows, :] = x_ref[...].astype(F32)

    def taps(start, n, valid=None):
        acc = b_ref[...]
        for k in range(CONV_KERNEL):
            xk = xs_scr[pl.ds(pad + start + k - half, n), :]
            if valid is not None:
                xk = jnp.where(valid[k], xk, 0.0)
            acc = acc + w_ref[pl.ds(k, 1), :] * xk
        return _silu(acc).astype(o_ref.dtype)

    o_ref[...] = taps(0, rows)

    @pl.when(pl.program_id(0) < n_prompt_tiles)
    def _():
        i = lax.broadcasted_iota(jnp.int32, (2 * pad, 1), 0)
        valid = [(i < pad) == (i + k - half < pad) for k in range(CONV_KERNEL)]
        for b in range(1, rows // len_prompt):
            r = b * len_prompt
            o_ref[r - pad:r + pad, :] = taps(r - pad, 2 * pad, valid)


def _conv_silu(proj, col0, width, conv_w, conv_b, n_prompt_rows, len_prompt, len_latent):
    m = proj.shape[0]
    rows = len_latent
    c0 = col0 // CONV_COLS
    return pl.pallas_call(
        functools.partial(_conv_kernel, n_prompt_tiles=n_prompt_rows // rows, len_prompt=len_prompt),
        grid=(m // rows, width // CONV_COLS),
        in_specs=[
            pl.BlockSpec((rows, CONV_COLS), lambda i, j: (i, c0 + j)),
            pl.BlockSpec((CONV_KERNEL, CONV_COLS), lambda i, j: (0, j)),
            pl.BlockSpec((1, CONV_COLS), lambda i, j: (0, j)),
        ],
        out_specs=pl.BlockSpec((rows, CONV_COLS), lambda i, j: (i, j)),
        out_shape=jax.ShapeDtypeStruct((m, width), BF16),
        scratch_shapes=[pltpu.VMEM((rows + 2 * CONV_PAD, CONV_COLS), F32)],
        compiler_params=_cparams(2),
        name="conv_silu",
    )(proj, conv_w, conv_b.reshape(1, -1))


def _ssd_kernel(*refs, seq_len, has_h0, emit_state, n_unread):
    x_ref, z_ref, b_ref, c_ref, dtt_ref, biast_ref, alogt_ref, dskip_ref = refs[:8]
    pos = 8
    h0_ref = None
    if has_h0:
        h0_ref = refs[pos]
        pos += 1
    pos += n_unread
    y_ref = refs[pos]
    pos += 1
    hfin_ref = None
    if emit_state:
        hfin_ref = refs[pos]
        pos += 1
    s_scr, dtct_scr, acst_scr, dtw_scr, acsw_scr, acsb_scr = refs[pos:pos + 6]

    q = A_CHUNK
    nc = seq_len // q
    hpg = A_HEADS_PER_GROUP
    hd = A_HEAD_DIM
    ii = lax.broadcasted_iota(jnp.int32, (q, q), 0)
    jj = lax.broadcasted_iota(jnp.int32, (q, q), 1)
    lower = ii >= jj
    upper = ii <= jj
    t_lower = jnp.where(lower, 1.0, 0.0).astype(BF16)
    t_upper = jnp.where(upper, 1.0, 0.0).astype(BF16)
    t_lower_r = jnp.concatenate([t_lower] * 3, axis=0)
    t_upper_r = jnp.concatenate([t_upper] * 3, axis=0)

    def expander(width):
        er = lax.broadcasted_iota(jnp.int32, (3 * hpg, hpg * width), 0)
        ec = lax.broadcasted_iota(jnp.int32, (3 * hpg, hpg * width), 1)
        return jnp.where(ec // width == er % hpg, 1.0, 0.0).astype(BF16)

    expand_hd = expander(hd)
    expand_q = expander(q)
    first_head = lax.broadcasted_iota(jnp.int32, (q, 2 * hd), 1) < hd

    def pieces(v_t):
        return jnp.concatenate([p.astype(F32) for p in _split3(v_t)], axis=0).astype(BF16)

    y_ref[...] = dskip_ref[...] * x_ref[...].astype(F32)

    for d in range(2):
        bias_t = biast_ref[d * hpg:(d + 1) * hpg, :]
        a_neg_t = -jnp.exp(alogt_ref[d * hpg:(d + 1) * hpg, :])
        sums = t_upper_r if d == 0 else t_lower_r
        dtcs = [_softplus(dtt_ref[c][d * hpg:(d + 1) * hpg, :] + bias_t) for c in range(nc)]
        dta = jnp.concatenate(dtcs, axis=0) * jnp.concatenate([a_neg_t] * nc, axis=0)
        acs = _dot(jnp.concatenate(_split3(dta), axis=1), sums)
        for c in range(nc):
            dtct_scr[d, c] = dtcs[c]
            acst_scr[d, c] = acs[c * hpg:(c + 1) * hpg]
        if has_h0:
            s_scr[d] = jnp.transpose(h0_ref[d].reshape(hpg * hd, A_D_STATE))
        else:
            s_scr[d] = jnp.zeros((A_D_STATE, hpg * hd), F32)

    def spread(step, slot):
        for d in range(2):
            c = step if d == 0 else nc - 1 - step
            acs_p = pieces(acst_scr[d, c])
            dtw_scr[slot, d] = _dot_tn(pieces(dtct_scr[d, c]), expand_hd)
            acsw_scr[slot, d] = _dot_tn(acs_p, expand_hd)
            acsb_scr[slot, d] = _dot_tn(acs_p, expand_q)

    spread(0, 0)

    def step(ci, slot):
        ts = []
        for d in range(2):
            c = ci if d == 0 else nc - 1 - ci
            r0 = pl.multiple_of(c * q, q)
            ts.append(dict(d=d, rows=pl.ds(r0, q), acs_t=acst_scr[d, c], mask=lower if d == 0 else upper,
                           dt_w=dtw_scr[slot, d], acs_w=acsw_scr[slot, d], acs_b=acsb_scr[slot, d]))
        for t in ts:
            t["bb"] = b_ref[t["rows"], :].astype(BF16)
            t["cc"] = c_ref[t["rows"], :].astype(BF16)
            t["cb"] = _dot_nt(t["cc"], t["bb"])
            t["s_old"] = s_scr[t["d"]]
            t["cs"] = _dot(t["cc"], t["s_old"].astype(BF16))
            t["xdt"] = x_ref[t["rows"], :].astype(F32) * t["dt_w"]
            t["xdt_b"] = t["xdt"].astype(BF16)
            t["tot_w"] = t["acs_w"][q - 1:q, :] if t["d"] == 0 else t["acs_w"][0:1, :]
        spread(jnp.minimum(ci + 1, nc - 1), 1 - slot)
        for t in ts:
            t["y_diag"] = []
        for pr in range(hpg // 2):
            for t in ts:
                mms = []
                for r in (2 * pr, 2 * pr + 1):
                    seg = t["acs_b"][:, r * q:(r + 1) * q] - t["acs_t"][r:r + 1, :]
                    lmat = jnp.exp(jnp.where(t["mask"], seg, -jnp.inf))
                    mms.append((t["cb"] * lmat).astype(BF16))
                x_pair = t["xdt_b"][:, 2 * pr * hd:2 * (pr + 1) * hd]
                x_diag = jnp.concatenate([jnp.where(first_head, x_pair, jnp.zeros_like(x_pair)),
                                          jnp.where(first_head, jnp.zeros_like(x_pair), x_pair)], axis=0)
                t["y_diag"].append(_dot(jnp.concatenate(mms, axis=1), x_diag))
        for t in ts:
            y_off = jnp.exp(t["acs_w"]) * t["cs"]
            y_ref[t["rows"], :] += jnp.concatenate(t["y_diag"], axis=1) + y_off
            xs = (t["xdt"] * jnp.exp(t["tot_w"] - t["acs_w"])).astype(BF16)
            s_scr[t["d"]] = t["s_old"] * jnp.exp(t["tot_w"]) + _dot_tn(t["bb"], xs)

    def body(k, carry):
        step(2 * k, 0)
        step(2 * k + 1, 1)
        return carry

    lax.fori_loop(0, nc // 2, body, 0)
    if nc % 2:
        step(nc - 1, 0)
    if emit_state:
        for d in range(2):
            hfin_ref[d] = jnp.transpose(s_scr[d]).reshape(hpg, hd, A_D_STATE)

    y_ref[...] = y_ref[...] * _silu(z_ref[...].astype(F32))


def _ssd_core(proj, dt_row, params, layer, n_layers, h0, prev, states_prev, row0, n_seq, seq_len, emit_state):
    m = proj.shape[0]
    gd = A_GROUP_DIM
    ns = A_D_STATE
    hpg = A_HEADS_PER_GROUP
    sb = row0 // seq_len
    nc = seq_len // A_CHUNK
    x_blocks = A_D_INNER // gd
    bias_r, alog_r, dskip = params
    in_specs = [
        pl.BlockSpec((seq_len, gd), lambda n, g: (sb + n, x_blocks + g)),
        pl.BlockSpec((seq_len, gd), lambda n, g: (sb + n, g)),
        pl.BlockSpec((seq_len, ns), lambda n, g: (sb + n, 2 * x_blocks * (gd // ns) + g)),
        pl.BlockSpec((seq_len, ns), lambda n, g: (sb + n, 2 * x_blocks * (gd // ns) + A_N_GROUPS + g)),
        pl.BlockSpec((None, nc, 2 * hpg, A_CHUNK), lambda n, g: (g, sb + n, 0, 0)),
        pl.BlockSpec((None, 2 * hpg, 1), lambda n, g: (g, 0, 0)),
        pl.BlockSpec((None, 2 * hpg, 1), lambda n, g: (g, 0, 0)),
        pl.BlockSpec((None, 1, gd), lambda n, g: (g, 0, 0)),
    ]
    args = [proj, proj, proj, proj, dt_row, bias_r, alog_r, dskip]
    aliases = {}
    state_spec = pl.BlockSpec((None, None, 2, hpg, A_HEAD_DIM, ns), lambda n, g: (n, layer, 0, g, 0, 0))
    if h0 is not None:
        in_specs.append(state_spec)
        args.append(h0)
    n_unread = 0
    for arr, out_idx in ((prev, 0), (states_prev, 1)):
        if arr is not None:
            in_specs.append(pl.BlockSpec(memory_space=pl.ANY))
            args.append(arr)
            aliases[len(args) - 1] = out_idx
            n_unread += 1
    out_specs = [pl.BlockSpec((seq_len, gd), lambda n, g: (sb + n, g))]
    out_shape = [jax.ShapeDtypeStruct((m, A_D_INNER), F32)]
    if emit_state:
        out_specs.append(state_spec)
        out_shape.append(jax.ShapeDtypeStruct((n_seq, n_layers, 2, A_N_HEADS, A_HEAD_DIM, ns), F32))
    res = pl.pallas_call(
        functools.partial(_ssd_kernel, seq_len=seq_len, has_h0=h0 is not None, emit_state=emit_state,
                          n_unread=n_unread),
        grid=(n_seq, A_N_GROUPS),
        in_specs=in_specs,
        out_specs=out_specs,
        out_shape=out_shape,
        scratch_shapes=[pltpu.VMEM((2, ns, gd), F32),
                        pltpu.VMEM((2, nc, hpg, A_CHUNK), F32),
                        pltpu.VMEM((2, nc, hpg, A_CHUNK), F32),
                        pltpu.VMEM((2, 2, A_CHUNK, gd), F32),
                        pltpu.VMEM((2, 2, A_CHUNK, gd), F32),
                        pltpu.VMEM((2, 2, A_CHUNK, hpg * A_CHUNK), F32)],
        input_output_aliases=aliases,
        compiler_params=_cparams(2),
        name="ssd_core",
    )(*args)
    return res if emit_state else (res[0], None)


def _ssd_mixer(y, mod, cond_of_tile, w_in, conv_w, conv_b, dt_bias, a_log, d_skip, layer, state0, states_prev,
               n_prompt, len_prompt, n_latent, len_latent):
    m = y.shape[0]
    mp = n_prompt * len_prompt
    proj = _mod_linear_conv(y, mod, w_in, conv_w, conv_b, A_D_INNER, cond_of_tile, mp, len_prompt,
                            len_latent)
    hpg = A_HEADS_PER_GROUP
    dt0 = A_D_INNER + A_CONV_DIM
    dt_raw = proj[:, dt0:dt0 + 2 * A_N_HEADS].astype(F32)
    dt_row = dt_raw.reshape(m // A_CHUNK, A_CHUNK, 2, A_N_GROUPS, hpg).transpose(3, 0, 2, 4, 1)
    dt_row = dt_row.reshape(A_N_GROUPS, m // A_CHUNK, 2 * hpg, A_CHUNK)

    def per_group(p):
        return p.reshape(2, A_N_GROUPS, hpg).transpose(1, 0, 2).reshape(A_N_GROUPS, 2 * hpg)

    bias_g = per_group(dt_bias.astype(F32))
    alog_g = per_group(a_log.astype(F32))
    dskip = jnp.repeat(d_skip.astype(F32), A_HEAD_DIM).reshape(A_N_GROUPS, 1, A_GROUP_DIM)
    params = (bias_g[:, :, None], alog_g[:, :, None], dskip)
    n_layers = state0.shape[1]
    out, st = _ssd_core(proj, dt_row, params, layer, n_layers, None, None, states_prev, 0, n_prompt,
                        len_prompt, True)
    out, _ = _ssd_core(proj, dt_row, params, layer, n_layers, state0.astype(F32), out, None, mp,
                       n_latent, len_latent, False)
    return out, st


def _gdn_kernel(*refs, seq_len, has_s0, emit_state):
    q_ref, k_ref, v_ref, z_ref, ab_ref, abt_ref, dtb_ref, alog_ref, ng_ref = refs[:9]
    pos = 9
    s0_ref = None
    if has_s0:
        s0_ref = refs[pos]
        pos += 2
    o_ref = refs[pos]
    pos += 1
    sfin_ref = None
    if emit_state:
        sfin_ref = refs[pos]
        pos += 1
    gq_scr, h_scr, et_scr, s_scr = refs[pos:pos + 4]

    hps = GDN_HEADS_PER_STEP
    hk, hv = B_HEAD_K, B_HEAD_V
    head0 = pl.program_id(1) * hps
    ch = B_CHUNK
    nc = seq_len // ch
    ii = lax.broadcasted_iota(jnp.int32, (ch, ch), 0)
    jj = lax.broadcasted_iota(jnp.int32, (ch, ch), 1)
    same_block = (ii // GDN_SOLVE_BLOCK) == (jj // GDN_SOLVE_BLOCK)
    dirs = []
    for d in range(2):
        dirs.append(dict(
            dt_bias=[dtb_ref[d, head0 + hh] for hh in range(hps)],
            a_neg=[-jnp.exp(alog_ref[d, head0 + hh]) for hh in range(hps)],
            incl=(ii >= jj) if d == 0 else (ii <= jj),
            incl_t=(ii <= jj) if d == 0 else (ii >= jj),
            strict=(ii > jj) if d == 0 else (ii < jj)))

    def mm(a, b):
        return _dot(a.astype(BF16), b.astype(BF16))

    def compose(a, b):
        return a + b + mm(a, b)

    n_prep = min(GDN_PREP_UNROLL, nc)

    def prepare(grp, carry):
        chains = []
        for j, hh in [(j, hh) for j in range(n_prep) for hh in range(hps)]:
            c = grp * n_prep + j
            rows = pl.ds(pl.multiple_of(c * ch, ch), ch)
            qc = q_ref[rows, hh * hk:(hh + 1) * hk].astype(F32)
            kc = k_ref[rows, hh * hk:(hh + 1) * hk].astype(F32)
            vc = v_ref[rows, hh * hv:(hh + 1) * hv].astype(F32)
            qn = qc * lax.rsqrt(jnp.sum(qc * qc, axis=-1, keepdims=True) + RMS_EPS) * (B_HEAD_K ** -0.5)
            kn = kc * lax.rsqrt(jnp.sum(kc * kc, axis=-1, keepdims=True) + RMS_EPS)
            knb = kn.astype(BF16)
            qk = _dot_nt(qn.astype(BF16), knb)
            kk = _dot_nt(knb, knb)
            ab = ab_ref[hh, rows, :]
            abt = abt_ref[hh, c]
            for d in range(2):
                p = dirs[d]
                a_neg, dt_bias = p["a_neg"][hh], p["dt_bias"][hh]
                beta = _sigmoid(ab[:, d:d + 1])
                g_col = a_neg * _softplus(ab[:, 2 + d:3 + d] + dt_bias)
                g_row = a_neg * _softplus(abt[2 + d:3 + d, :] + dt_bias)
                gcs = jnp.sum(jnp.where(p["incl"], jnp.broadcast_to(g_row, (ch, ch)), 0.0), axis=1, keepdims=True)
                gcs_t = jnp.sum(jnp.where(p["incl_t"], jnp.broadcast_to(g_col, (ch, ch)), 0.0), axis=0,
                                keepdims=True)
                tot = gcs[ch - 1:ch, :] if d == 0 else gcs[0:1, :]
                decay = jnp.exp(jnp.where(p["incl"], gcs - gcs_t, -jnp.inf))
                a_mat = jnp.where(p["strict"], beta * kk * decay, 0.0)
                a_diag = jnp.where(same_block, a_mat, 0.0)
                e_gcs = jnp.exp(gcs)
                chains.append(dict(
                    d=d, c=c, hh=hh, rows=rows, a_off=a_mat - a_diag, dx=-a_diag, pw=a_diag,
                    wu=jnp.concatenate([kn * (beta * e_gcs), vc * beta], axis=1),
                    kd=kn * jnp.exp(tot - gcs), qd=qn * e_gcs, qkm=(qk * decay).astype(BF16),
                    et=jnp.broadcast_to(jnp.exp(tot), (1, B_HEAD_V))))
        for _ in range(3):
            for t in chains:
                t["pw"] = mm(t["pw"], t["pw"])
            for t in chains:
                t["dx"] = compose(t["dx"], t["pw"])
        for t in chains:
            t["n"] = t["a_off"] + mm(t["dx"], t["a_off"])
        for t in chains:
            t["nn"] = mm(t["n"], t["n"])
        for t in chains:
            t["mx"] = compose(-t["n"], t["nn"])
        for t in chains:
            t["xb"] = compose(t["mx"], t["dx"]).astype(BF16)
        for t in chains:
            t["wu"] = (t["wu"] + _dot(t["xb"], t["wu"].astype(BF16))).astype(BF16)
        for t in chains:
            t["gh"] = _dot(jnp.transpose(t["kd"]).astype(BF16), t["wu"])
            t["qo"] = _dot(t["qkm"], t["wu"])
        for t in chains:
            d, c, hh = t["d"], t["c"], t["hh"]
            gq_scr[d, hh, c, 0:hk, :] = t["gh"][:, :hk].astype(BF16)
            gq_scr[d, hh, c, hk:, :] = (t["qd"] - t["qo"][:, :hk]).astype(BF16)
            h_scr[d, hh, c] = t["gh"][:, hk:]
            et_scr[d, hh, c] = t["et"]
        for fwd, bwd in zip(chains[0::2], chains[1::2]):
            hh = fwd["hh"]
            o_ref[fwd["rows"], hh * hv:(hh + 1) * hv] = fwd["qo"][:, hk:] + bwd["qo"][:, hk:]
        return carry

    lax.fori_loop(0, nc // n_prep, prepare, 0)

    for d in range(2):
        for hh in range(hps):
            if has_s0:
                s_scr[d, hh] = s0_ref[d, hh]
            else:
                s_scr[d, hh] = jnp.zeros((hk, hv), F32)

    def recur(ci, carry):
        olds = {}
        prods = {}
        for d in range(2):
            c = ci if d == 0 else nc - 1 - ci
            for hh in range(hps):
                olds[d, hh] = s_scr[d, hh]
                prods[d, hh] = _dot(gq_scr[d, hh, c], olds[d, hh].astype(BF16))
        for d in range(2):
            c = ci if d == 0 else nc - 1 - ci
            rows = pl.ds(pl.multiple_of(c * ch, ch), ch)
            for hh in range(hps):
                r = prods[d, hh]
                s_scr[d, hh] = olds[d, hh] * et_scr[d, hh, c] - r[:hk] + h_scr[d, hh, c]
                o_ref[rows, hh * hv:(hh + 1) * hv] += r[hk:]
        return carry

    lax.fori_loop(0, nc, recur, 0)
    if emit_state:
        for d in range(2):
            for hh in range(hps):
                sfin_ref[d, hh] = s_scr[d, hh]

    for hh in range(hps):
        cols = slice(hh * hv, (hh + 1) * hv)
        o = o_ref[:, cols]
        o = o * lax.rsqrt(jnp.mean(o * o, axis=-1, keepdims=True) + RMS_EPS) * ng_ref[...]
        o_ref[:, cols] = o * _silu(z_ref[:, cols].astype(F32))


def _gdn_core(proj, qkv, ab_col, ab_row, dt_bias, a_log, norm_g, s0, prev, row0, n_seq, seq_len, emit_state):
    m = proj.shape[0]
    hk, hv = B_HEAD_K, B_HEAD_V
    hps = GDN_HEADS_PER_STEP
    wk, wv = hps * hk, hps * hv
    sb = row0 // seq_len
    nc = seq_len // B_CHUNK
    in_specs = [
        pl.BlockSpec((seq_len, wk), lambda n, h: (sb + n, h)),
        pl.BlockSpec((seq_len, wk), lambda n, h: (sb + n, B_QK_DIM // wk + h)),
        pl.BlockSpec((seq_len, wv), lambda n, h: (sb + n, (2 * B_QK_DIM) // wv + h)),
        pl.BlockSpec((seq_len, wv), lambda n, h: (sb + n, B_CONV_DIM // wv + h)),
        pl.BlockSpec((hps, seq_len, 4), lambda n, h: (h, sb + n, 0)),
        pl.BlockSpec((hps, nc, 4, B_CHUNK), lambda n, h: (h, sb + n, 0, 0)),
        pl.BlockSpec(memory_space=pltpu.SMEM),
        pl.BlockSpec(memory_space=pltpu.SMEM),
        pl.BlockSpec((1, hv), lambda n, h: (0, 0)),
    ]
    args = [qkv, qkv, qkv, proj, ab_col, ab_row, dt_bias, a_log, norm_g.reshape(1, hv)]
    aliases = {}
    if s0 is not None:
        in_specs += [pl.BlockSpec((None, 2, hps, hk, hv), lambda n, h: (n, 0, h, 0, 0)),
                     pl.BlockSpec(memory_space=pl.ANY)]
        args += [s0, prev]
        aliases = {len(args) - 1: 0}
    out_specs = [pl.BlockSpec((seq_len, wv), lambda n, h: (sb + n, h))]
    out_shape = [jax.ShapeDtypeStruct((m, B_V_DIM), F32)]
    if emit_state:
        out_specs.append(pl.BlockSpec((None, 2, hps, hk, hv), lambda n, h: (n, 0, h, 0, 0)))
        out_shape.append(jax.ShapeDtypeStruct((n_seq, 2, B_N_HEADS, hk, hv), F32))
    res = pl.pallas_call(
        functools.partial(_gdn_kernel, seq_len=seq_len, has_s0=s0 is not None, emit_state=emit_state),
        grid=(n_seq, B_N_HEADS // hps),
        in_specs=in_specs,
        out_specs=out_specs,
        out_shape=out_shape,
        scratch_shapes=[
            pltpu.VMEM((2, hps, nc, hk + B_CHUNK, hk), BF16),
            pltpu.VMEM((2, hps, nc, hk, hv), F32),
            pltpu.VMEM((2, hps, nc, 1, hv), F32),
            pltpu.VMEM((2, hps, hk, hv), F32),
        ],
        input_output_aliases=aliases,
        compiler_params=_cparams(2),
        name="gdn_core",
    )(*args)
    return res if emit_state else (res[0], None)


def _gdn_mixer(y, mod, cond_of_tile, w_in, conv_w, conv_b, dt_bias, a_log, norm_g, state0, n_prompt, len_prompt,
               n_latent, len_latent):
    m = y.shape[0]
    mp = n_prompt * len_prompt
    proj = _mod_linear_conv(y, mod, w_in, conv_w, conv_b, 0, cond_of_tile, mp, len_prompt,
                            len_latent)
    qkv = proj
    ab0 = B_CONV_DIM + B_V_DIM
    ab = proj[:, ab0:ab0 + 4 * B_N_HEADS].astype(F32).reshape(m, 4, B_N_HEADS)
    ab_col = ab.transpose(2, 0, 1)
    ab_row = ab.reshape(m // B_CHUNK, B_CHUNK, 4, B_N_HEADS).transpose(3, 0, 2, 1)
    dt_bias = dt_bias.astype(F32)
    a_log = a_log.astype(F32)
    out, st = _gdn_core(proj, qkv, ab_col, ab_row, dt_bias, a_log, norm_g, None, None, 0, n_prompt, len_prompt, True)
    out, _ = _gdn_core(proj, qkv, ab_col, ab_row, dt_bias, a_log, norm_g, state0, out, mp, n_latent, len_latent,
                       False)
    return out, st


KV_PAIR = 2
PAIR_Q = KV_PAIR * C_GROUP * C_HEAD_DIM
PAIR_KV = KV_PAIR * C_HEAD_DIM


def _softmax_pv(heads, values):
    mxs = []
    for scores, sink in heads:
        mx = sink
        for s in scores:
            mx = jnp.maximum(mx, jnp.max(s, axis=-1, keepdims=True))
        mxs.append(mx)
    probs = [[jnp.exp(s - mx) for s in scores] for (scores, _), mx in zip(heads, mxs)]
    dens = []
    for (_, sink), mx, ps in zip(heads, mxs, probs):
        den = jnp.exp(sink - mx)
        for p in ps:
            den = den + jnp.sum(p, axis=-1, keepdims=True)
        dens.append(den)
    outs = []
    for ps in probs:
        acc = None
        for p, v in zip(ps, values):
            pv = _dot(p.astype(BF16), v)
            acc = pv if acc is None else acc + pv
        outs.append(acc)
    return [acc / den for acc, den in zip(outs, dens)]


def _attn_ctx_kernel(q_ref, k_ref, v_ref, sink_ref, o_ref):
    pair = pl.program_id(1)
    hd = C_HEAD_DIM
    for gg in range(KV_PAIR):
        kb = k_ref[:, gg * hd:(gg + 1) * hd].astype(BF16)
        vb = v_ref[:, gg * hd:(gg + 1) * hd].astype(BF16)
        heads = []
        for r in range(C_GROUP):
            col = (gg * C_GROUP + r) * hd
            qb = q_ref[:, col:col + hd].astype(BF16)
            sink = sink_ref[pair * (KV_PAIR * C_GROUP) + gg * C_GROUP + r]
            heads.append(([_dot_nt(qb, kb) * C_SCALE], sink))
        for r, o in enumerate(_softmax_pv(heads, [vb])):
            col = (gg * C_GROUP + r) * hd
            o_ref[:, col:col + hd] = o


def _attn_ctx(proj, sink, n_prompt, len_prompt):
    m = proj.shape[0]
    k_block0 = C_Q_DIM // PAIR_KV
    v_block0 = (C_Q_DIM + C_KV_DIM) // PAIR_KV
    return pl.pallas_call(
        _attn_ctx_kernel,
        grid=(n_prompt, C_N_KV // KV_PAIR),
        in_specs=[
            pl.BlockSpec((len_prompt, PAIR_Q), lambda n, p: (n, p)),
            pl.BlockSpec((len_prompt, PAIR_KV), lambda n, p: (n, k_block0 + p)),
            pl.BlockSpec((len_prompt, PAIR_KV), lambda n, p: (n, v_block0 + p)),
            pl.BlockSpec(memory_space=pltpu.SMEM),
        ],
        out_specs=pl.BlockSpec((len_prompt, PAIR_Q), lambda n, p: (n, p)),
        out_shape=jax.ShapeDtypeStruct((m, C_Q_DIM), F32),
        compiler_params=_cparams(2),
        name="attn_ctx",
    )(proj, proj, proj, sink)


def _rope(x, cos, sin):
    width = x.shape[1]
    lane = lax.broadcasted_iota(jnp.int32, x.shape, 1)
    nf = C_HEAD_DIM // 4
    partner = jnp.where(lane % (2 * nf) < nf, pltpu.roll(x, width - nf, 1), pltpu.roll(x, nf, 1))
    return x * cos + partner * sin


def _attn_lat_kernel(q_ref, kp_ref, vp_ref, kc_ref, vc_ref, cq_ref, sq_ref, ck_ref, sk_ref, sink_ref, prev_ref, o_ref,
                     *, seq_len):
    del prev_ref
    pair = pl.program_id(1)
    qi = pl.program_id(2)
    hd = C_HEAD_DIM
    blk = C_BLOCK
    span = C_BLOCK + 2 * C_WINDOW
    start = pl.multiple_of(qi * blk, blk)
    q = _rope(q_ref[...].astype(F32), cq_ref[...], sq_ref[...])
    kw = _rope(kp_ref[pl.ds(start, span), :].astype(F32), ck_ref[pl.ds(start, span), :],
               sk_ref[pl.ds(start, span), :])
    vw = vp_ref[pl.ds(start, span), :]
    iq = lax.broadcasted_iota(jnp.int32, (blk, span), 0)
    jk = lax.broadcasted_iota(jnp.int32, (blk, span), 1)
    kpos = start - C_WINDOW + jk
    ok = jnp.logical_and(jnp.abs(iq + C_WINDOW - jk) <= C_WINDOW, jnp.logical_and(kpos >= 0, kpos < seq_len))
    for gg in range(KV_PAIR):
        kb = kw[:, gg * hd:(gg + 1) * hd].astype(BF16)
        vb = vw[:, gg * hd:(gg + 1) * hd].astype(BF16)
        kcb = kc_ref[:, gg * hd:(gg + 1) * hd].astype(BF16)
        vcb = vc_ref[:, gg * hd:(gg + 1) * hd].astype(BF16)
        heads = []
        for r in range(C_GROUP):
            col = (gg * C_GROUP + r) * hd
            qb = q[:, col:col + hd].astype(BF16)
            s_loc = jnp.where(ok, _dot_nt(qb, kb) * C_SCALE, -jnp.inf)
            s_ctx = _dot_nt(qb, kcb) * C_SCALE
            sink = sink_ref[pair * (KV_PAIR * C_GROUP) + gg * C_GROUP + r]
            heads.append(([s_loc, s_ctx], sink))
        for r, o in enumerate(_softmax_pv(heads, [vb, vcb])):
            col = (gg * C_GROUP + r) * hd
            o_ref[:, col:col + hd] = o


def _rope_tables(seq_len):
    nf = C_HEAD_DIM // 4
    t = jnp.arange(seq_len)
    inv_freq = ROPE_BASE ** (-jnp.arange(nf, dtype=F32) / nf)
    ang_r = (t // GRID_W).astype(F32)[:, None] * inv_freq
    ang_c = (t % GRID_W).astype(F32)[:, None] * inv_freq
    cos = jnp.concatenate([jnp.cos(ang_r), jnp.cos(ang_r), jnp.cos(ang_c), jnp.cos(ang_c)], axis=1)
    sin = jnp.concatenate([-jnp.sin(ang_r), jnp.sin(ang_r), -jnp.sin(ang_c), jnp.sin(ang_c)], axis=1)
    return cos, sin


def _attn_lat(proj, cache_k, cache_v, sink, prev, row0, n_latent, len_latent):
    m = proj.shape[0]
    past = cache_k.shape[1]
    w = C_WINDOW
    k_lat = proj[row0:, C_Q_DIM:C_Q_DIM + C_KV_DIM].reshape(n_latent, len_latent, C_KV_DIM)
    v_lat = proj[row0:, C_Q_DIM + C_KV_DIM:C_Q_DIM + 2 * C_KV_DIM].reshape(n_latent, len_latent, C_KV_DIM)
    kp = jnp.pad(k_lat, ((0, 0), (w, w), (0, 0)))
    vp = jnp.pad(v_lat, ((0, 0), (w, w), (0, 0)))
    kc = cache_k.reshape(n_latent, past, C_KV_DIM).astype(F32)
    vc = cache_v.reshape(n_latent, past, C_KV_DIM).astype(F32)
    cos, sin = _rope_tables(len_latent)
    cos_q = jnp.tile(cos, (1, PAIR_Q // C_HEAD_DIM))
    sin_q = jnp.tile(sin, (1, PAIR_Q // C_HEAD_DIM))
    cos_k = jnp.pad(jnp.tile(cos, (1, KV_PAIR)), ((w, w), (0, 0)))
    sin_k = jnp.pad(jnp.tile(sin, (1, KV_PAIR)), ((w, w), (0, 0)))
    qb0 = row0 // C_BLOCK
    nqb = len_latent // C_BLOCK
    padded = len_latent + 2 * w
    return pl.pallas_call(
        functools.partial(_attn_lat_kernel, seq_len=len_latent),
        grid=(n_latent, C_N_KV // KV_PAIR, nqb),
        in_specs=[
            pl.BlockSpec((C_BLOCK, PAIR_Q), lambda n, p, i: (qb0 + n * nqb + i, p)),
            pl.BlockSpec((None, padded, PAIR_KV), lambda n, p, i: (n, 0, p)),
            pl.BlockSpec((None, padded, PAIR_KV), lambda n, p, i: (n, 0, p)),
            pl.BlockSpec((None, past, PAIR_KV), lambda n, p, i: (n, 0, p)),
            pl.BlockSpec((None, past, PAIR_KV), lambda n, p, i: (n, 0, p)),
            pl.BlockSpec((C_BLOCK, PAIR_Q), lambda n, p, i: (i, 0)),
            pl.BlockSpec((C_BLOCK, PAIR_Q), lambda n, p, i: (i, 0)),
            pl.BlockSpec((padded, PAIR_KV), lambda n, p, i: (0, 0)),
            pl.BlockSpec((padded, PAIR_KV), lambda n, p, i: (0, 0)),
            pl.BlockSpec(memory_space=pltpu.SMEM),
            pl.BlockSpec(memory_space=pl.ANY),
        ],
        out_specs=pl.BlockSpec((C_BLOCK, PAIR_Q), lambda n, p, i: (qb0 + n * nqb + i, p)),
        out_shape=jax.ShapeDtypeStruct((m, C_Q_DIM), F32),
        input_output_aliases={10: 0},
        compiler_params=_cparams(3),
        name="attn_lat",
    )(proj, kp, vp, kc, vc, cos_q, sin_q, cos_k, sin_k, sink, prev)


def _attn_mixer(y, mod, cond_of_tile, w_in, sink, cache_k, cache_v, n_prompt, len_prompt, n_latent, len_latent):
    m = y.shape[0]
    mp = n_prompt * len_prompt
    proj = _mod_linear(y, mod, w_in, cond_of_tile)
    sink = sink.astype(F32)
    out = _attn_ctx(proj, sink, n_prompt, len_prompt)
    out = _attn_lat(proj, cache_k, cache_v, sink, out, mp, n_latent, len_latent)
    new_k = proj[:mp, C_Q_DIM:C_Q_DIM + C_KV_DIM].astype(F32).reshape(n_prompt, len_prompt, C_N_KV, C_HEAD_DIM)
    new_v = proj[:mp, C_Q_DIM + C_KV_DIM:C_Q_DIM + 2 * C_KV_DIM].astype(F32)
    new_v = new_v.reshape(n_prompt, len_prompt, C_N_KV, C_HEAD_DIM)
    return out, new_k, new_v


def kernel(x_prompt, x_sample, state_ssd, state_delta, cache_k, cache_v, c, c_ctx, w_mod, b_mod, ln_g, ln_b, ffn_w_gate, ffn_w_up, ffn_w_down, ssd_w_in, ssd_conv_w, ssd_conv_b, ssd_dt_bias, ssd_a_log, ssd_d, ssd_norm, ssd_w_out, gdn_w_in, gdn_conv_w, gdn_conv_b, gdn_dt_bias, gdn_a_log, gdn_norm, gdn_w_out, attn_w_in, attn_sink, attn_w_out):
    n_prompt, len_prompt, d = x_prompt.shape
    n_latent, len_latent, _ = x_sample.shape
    mp = n_prompt * len_prompt
    assert d == D_MODEL and n_latent + 1 <= N_COND
    assert len_prompt % ROW_TILE == 0 or ROW_TILE % len_prompt == 0
    assert mp % ROW_TILE == 0 and len_latent % ROW_TILE == 0
    assert mp % len_latent == 0 and len_latent % len_prompt == 0 and len_latent % GRID_W == 0

    assert mp % FFN_ROW_TILE == 0 and len_latent % FFN_ROW_TILE == 0

    def cond_of_tile(i, tile_rows=ROW_TILE):
        return _cond_index(i, mp // tile_rows, len_latent // tile_rows)

    cond = jnp.concatenate([c_ctx[None].astype(F32), c.astype(F32),
                            jnp.zeros((N_COND - 1 - n_latent, d), F32)], axis=0)
    mods = _adaln(cond, w_mod, b_mod).reshape(DEPTH, N_COND, N_MOD, d)

    y = (x_prompt.reshape(mp, d).astype(F32), x_sample.reshape(n_latent * len_latent, d).astype(F32))
    w_gate, w_up, w_down = ffn_w_gate.astype(BF16), ffn_w_up.astype(BF16), ffn_w_down.astype(BF16)
    seqs = (n_prompt, len_prompt, n_latent, len_latent)
    ssd_states, gdn_states, k_list, v_list = None, [], [], []
    for i in range(DEPTH):
        mod = mods[i]
        y = _ffn_half(y, mod, 0, ln_g[i, 0], ln_b[i, 0], w_gate, w_up, w_down, i, 0, cond_of_tile, mp)
        kind, j = i % 3, i // 3
        if kind == 0:
            a, ssd_states = _ssd_mixer(y, mod, cond_of_tile, ssd_w_in[j], ssd_conv_w[j], ssd_conv_b[j],
                                       ssd_dt_bias[j], ssd_a_log[j], ssd_d[j], j, state_ssd, ssd_states, *seqs)
            y = _out_proj_ln(a, y, mod, ssd_w_out[j], ln_g[i, 1], ln_b[i, 1], cond_of_tile, rms_g=ssd_norm[j])
        elif kind == 1:
            a, st = _gdn_mixer(y, mod, cond_of_tile, gdn_w_in[j], gdn_conv_w[j], gdn_conv_b[j], gdn_dt_bias[j],
                               gdn_a_log[j], gdn_norm[j], state_delta[:, j], *seqs)
            gdn_states.append(st)
            y = _out_proj_ln(a, y, mod, gdn_w_out[j], ln_g[i, 1], ln_b[i, 1], cond_of_tile)
        else:
            a, kc, vc = _attn_mixer(y, mod, cond_of_tile, attn_w_in[j], attn_sink[j], cache_k[:, j], cache_v[:, j],
                                    *seqs)
            k_list.append(kc)
            v_list.append(vc)
            y = _out_proj_ln(a, y, mod, attn_w_out[j], ln_g[i, 1], ln_b[i, 1], cond_of_tile)
        y = _ffn_half(y, mod, 2, ln_g[i, 2], ln_b[i, 2], w_gate, w_up, w_down, i, 1, cond_of_tile, mp,
                      split_out=i == DEPTH - 1)

    y_prompt = y[0].reshape(n_prompt, len_prompt, d)
    y_sample = y[1].reshape(n_latent, len_latent, d)
    return (y_prompt, y_sample, ssd_states, jnp.stack(gdn_states, axis=1),
            jnp.stack(k_list, axis=1), jnp.stack(v_list, axis=1))
```

```python
import functools
import math

import jax
import jax.numpy as jnp
from jax import lax
from jax.experimental import pallas as pl
from jax.experimental.pallas import tpu as pltpu

F32 = jnp.float32
BF16 = jnp.bfloat16

D_MODEL = 1024
DEPTH = 4
GRID_W = 64
N_MOD = 9
D_FF = 2816
DEEPNORM_ALPHA = (2 * DEPTH) ** 0.25
LN_EPS = 1e-5
RMS_EPS = 1e-6
FFN_RES = 0.5

A_D_INNER = 2 * D_MODEL
A_HEAD_DIM = 64
A_N_HEADS = A_D_INNER // A_HEAD_DIM
A_N_GROUPS = 4
A_HEADS_PER_GROUP = A_N_HEADS // A_N_GROUPS
A_GROUP_DIM = A_HEADS_PER_GROUP * A_HEAD_DIM
A_D_STATE = 128
A_CHUNK = 128
A_CONV_DIM = A_D_INNER + 2 * A_N_GROUPS * A_D_STATE

B_N_HEADS = 8
B_HEAD_K = 128
B_HEAD_V = 256
B_CHUNK = 64
B_QK_DIM = B_N_HEADS * B_HEAD_K
B_V_DIM = B_N_HEADS * B_HEAD_V
B_CONV_DIM = 2 * B_QK_DIM + B_V_DIM
GDN_SOLVE_BLOCK = 16
GDN_PREP_UNROLL = 4
GDN_HEADS_PER_STEP = 2

C_N_HEADS = 16
C_N_KV = 4
C_GROUP = C_N_HEADS // C_N_KV
C_HEAD_DIM = 64
C_WINDOW = 128
C_BLOCK = 128
C_Q_DIM = C_N_HEADS * C_HEAD_DIM
C_KV_DIM = C_N_KV * C_HEAD_DIM
C_SCALE = C_HEAD_DIM ** -0.5
ROPE_BASE = 10000.0

N_COND = 16
ROW_TILE = 512
PROJ_TILE_MAX = 3328
PROJ_ALIGN = 256
FF_TILE = 1408
FFN_ROW_TILE = 512
FFN_SUBTILES = 2
CONV_KERNEL = 5
CONV_PAD = 16
CONV_COLS = 256
VMEM_LIMIT = 56 * 1024 * 1024


def _cparams(n_axes):
    return pltpu.CompilerParams(dimension_semantics=("arbitrary",) * n_axes, vmem_limit_bytes=VMEM_LIMIT)


def _dot(a, b, precision=None):
    return lax.dot_general(a, b, (((1,), (0,)), ((), ())), precision=precision, preferred_element_type=F32)


def _dot_nt(a, b, precision=None):
    return lax.dot_general(a, b, (((1,), (1,)), ((), ())), precision=precision, preferred_element_type=F32)


def _dot_tn(a, b, precision=None):
    return lax.dot_general(a, b, (((0,), (0,)), ((), ())), precision=precision, preferred_element_type=F32)


def _split3(x):
    hi = x.astype(BF16)
    rest = x - hi.astype(F32)
    mid = rest.astype(BF16)
    lo = (rest - mid.astype(F32)).astype(BF16)
    return [hi, mid, lo]


def _sigmoid(x):
    return 1.0 / (1.0 + jnp.exp(-x))


def _silu(x):
    return x * _sigmoid(x)


def _softplus(x):
    return jnp.maximum(x, 0.0) + jnp.log(1.0 + jnp.exp(-jnp.abs(x)))


def _layer_norm(t, g, b):
    mu = jnp.mean(t, axis=-1, keepdims=True)
    tc = t - mu
    var = jnp.mean(tc * tc, axis=-1, keepdims=True)
    return tc * lax.rsqrt(var + LN_EPS) * g + b


def _cond_index(i, n_prompt_tiles, tiles_per_request):
    return jnp.where(i < n_prompt_tiles, 0, 1 + (jnp.maximum(i - n_prompt_tiles, 0)) // tiles_per_request)


def _adaln_kernel(c_ref, w_ref, b_ref, o_ref):
    c = c_ref[...]
    h = _silu(c).astype(BF16)
    o_ref[...] = _dot(h, w_ref[...].astype(BF16)) + b_ref[...]


def _adaln(cond, w_mod, b_mod):
    n_out = N_MOD * D_MODEL
    tn = D_MODEL
    return pl.pallas_call(
        _adaln_kernel,
        grid=(DEPTH, n_out // tn),
        in_specs=[
            pl.BlockSpec((N_COND, D_MODEL), lambda l, j: (0, 0)),
            pl.BlockSpec((None, D_MODEL, tn), lambda l, j: (l, 0, j)),
            pl.BlockSpec((None, 1, tn), lambda l, j: (l, 0, j)),
        ],
        out_specs=pl.BlockSpec((None, N_COND, tn), lambda l, j: (l, 0, j)),
        out_shape=jax.ShapeDtypeStruct((DEPTH, N_COND, n_out), F32),
        compiler_params=_cparams(2),
        name="adaln",
    )(cond, w_mod, b_mod.reshape(DEPTH, 1, n_out))


def _ffn_kernel(*refs, s, n_in, n_out, n_prompt_tiles):
    y_refs = refs[:n_in]
    mod_ref, g_ref, b_ref, wg_ref, wu_ref, wd_ref = refs[n_in:n_in + 6]
    o_refs = refs[n_in + 6:]
    is_prompt = pl.program_id(0) < n_prompt_tiles
    shift = mod_ref[pl.ds(3 * s, 1), :]
    scale = mod_ref[pl.ds(3 * s + 1, 1), :]
    gate = mod_ref[pl.ds(3 * s + 2, 1), :]
    sub = FFN_ROW_TILE // FFN_SUBTILES
    for r in range(FFN_SUBTILES):
        rows = slice(r * sub, (r + 1) * sub)
        if n_in == 1:
            y = y_refs[0][rows, :]
        else:
            y = jnp.where(is_prompt, y_refs[0][rows, :], y_refs[1][rows, :])
        h = (y * (1.0 + scale) + shift).astype(BF16)
        acc = None
        for c in range(D_FF // FF_TILE):
            cols = slice(c * FF_TILE, (c + 1) * FF_TILE)
            a = _dot(h, wg_ref[:, cols])
            u = _dot(h, wu_ref[:, cols])
            f = (_silu(a) * u).astype(BF16)
            part = _dot(f, wd_ref[cols, :])
            acc = part if acc is None else acc + part
        t = DEEPNORM_ALPHA * y + (FFN_RES * gate) * acc
        res = _layer_norm(t, g_ref[...], b_ref[...])
        if n_out == 1:
            o_refs[0][rows, :] = res
        else:
            @pl.when(is_prompt)
            def _(res=res, rows=rows):
                o_refs[0][rows, :] = res

            @pl.when(jnp.logical_not(is_prompt))
            def _(res=res, rows=rows):
                o_refs[1][rows, :] = res


def _ffn_half(ys, mod, s, g, b, w_gate, w_up, w_down, layer, half, cond_of_tile, n_prompt_rows, split_out=False):
    npt = n_prompt_rows // FFN_ROW_TILE
    split_in = isinstance(ys, (tuple, list))
    ys = tuple(ys) if split_in else (ys,)
    m = sum(a.shape[0] for a in ys)
    resident = pl.Buffered(1)

    def whole(i):
        return (i, 0)

    def prompt_part(i):
        return (jnp.minimum(i, npt - 1), 0)

    def latent_part(i):
        return (jnp.maximum(i - npt, 0), 0)

    def row_specs(split):
        return [pl.BlockSpec((FFN_ROW_TILE, D_MODEL), f)
                for f in ((prompt_part, latent_part) if split else (whole,))]

    if split_out:
        out_shape = [jax.ShapeDtypeStruct((n_prompt_rows, D_MODEL), F32),
                     jax.ShapeDtypeStruct((m - n_prompt_rows, D_MODEL), F32)]
    else:
        out_shape = [jax.ShapeDtypeStruct((m, D_MODEL), F32)]
    res = pl.pallas_call(
        functools.partial(_ffn_kernel, s=s, n_in=len(ys), n_out=len(out_shape), n_prompt_tiles=npt),
        grid=(m // FFN_ROW_TILE,),
        in_specs=row_specs(split_in) + [
            pl.BlockSpec((None, N_MOD, D_MODEL), lambda i: (cond_of_tile(i, FFN_ROW_TILE), 0, 0)),
            pl.BlockSpec((1, D_MODEL), lambda i: (0, 0)),
            pl.BlockSpec((1, D_MODEL), lambda i: (0, 0)),
            pl.BlockSpec((None, None, D_MODEL, D_FF), lambda i: (layer, half, 0, 0), pipeline_mode=resident),
            pl.BlockSpec((None, None, D_MODEL, D_FF), lambda i: (layer, half, 0, 0), pipeline_mode=resident),
            pl.BlockSpec((None, None, D_FF, D_MODEL), lambda i: (layer, half, 0, 0), pipeline_mode=resident),
        ],
        out_specs=row_specs(split_out),
        out_shape=out_shape,
        compiler_params=_cparams(1),
        name="ffn_half",
    )(*ys, mod, g.reshape(1, -1), b.reshape(1, -1), w_gate, w_up, w_down)
    return tuple(res) if split_out else res[0]


def _modlin_kernel(y_ref, mod_ref, w_ref, o_ref):
    shift = mod_ref[pl.ds(3, 1), :]
    scale = mod_ref[pl.ds(4, 1), :]
    h = (y_ref[...] * (1.0 + scale) + shift).astype(BF16)
    o_ref[...] = _dot(h, w_ref[...]).astype(o_ref.dtype)


def _mod_linear(y, mod, w, cond_of_tile):
    m = y.shape[0]
    n = w.shape[1]
    n_tiles = -(-n // PROJ_TILE_MAX)
    tn = -(-n // (n_tiles * PROJ_ALIGN)) * PROJ_ALIGN
    n_pad = n_tiles * tn
    w = jnp.pad(w.astype(BF16), ((0, 0), (0, n_pad - n)))
    return pl.pallas_call(
        _modlin_kernel,
        grid=(n_tiles, m // ROW_TILE),
        in_specs=[
            pl.BlockSpec((ROW_TILE, D_MODEL), lambda j, i: (i, 0)),
            pl.BlockSpec((None, N_MOD, D_MODEL), lambda j, i: (cond_of_tile(i), 0, 0)),
            pl.BlockSpec((D_MODEL, tn), lambda j, i: (0, j)),
        ],
        out_specs=pl.BlockSpec((ROW_TILE, tn), lambda j, i: (i, j)),
        out_shape=jax.ShapeDtypeStruct((m, n_pad), BF16),
        compiler_params=_cparams(2),
        name="mod_linear",
    )(y, mod, w)


def _outproj_kernel(a_ref, y_ref, mod_ref, w_ref, g_ref, b_ref, *rest, rms):
    if rms:
        ng_ref, o_ref = rest
    else:
        (o_ref,) = rest
    a = a_ref[...]
    if rms:
        a = a.astype(F32)
        a = a * lax.rsqrt(jnp.mean(a * a, axis=-1, keepdims=True) + RMS_EPS) * ng_ref[...]
    mix = _dot(a.astype(BF16), w_ref[...])
    gate = mod_ref[pl.ds(5, 1), :]
    t = DEEPNORM_ALPHA * y_ref[...] + gate * mix
    o_ref[...] = _layer_norm(t, g_ref[...], b_ref[...])


def _out_proj_ln(a, y, mod, w, g, b, cond_of_tile, rms_g=None):
    m, k = a.shape
    in_specs = [
        pl.BlockSpec((ROW_TILE, k), lambda i: (i, 0)),
        pl.BlockSpec((ROW_TILE, D_MODEL), lambda i: (i, 0)),
        pl.BlockSpec((None, N_MOD, D_MODEL), lambda i: (cond_of_tile(i), 0, 0)),
        pl.BlockSpec((k, D_MODEL), lambda i: (0, 0)),
        pl.BlockSpec((1, D_MODEL), lambda i: (0, 0)),
        pl.BlockSpec((1, D_MODEL), lambda i: (0, 0)),
    ]
    args = [a, y, mod, w.astype(BF16), g.reshape(1, -1), b.reshape(1, -1)]
    if rms_g is not None:
        in_specs.append(pl.BlockSpec((1, k), lambda i: (0, 0)))
        args.append(rms_g.reshape(1, -1))
    return pl.pallas_call(
        functools.partial(_outproj_kernel, rms=rms_g is not None),
        grid=(m // ROW_TILE,),
        in_specs=in_specs,
        out_specs=pl.BlockSpec((ROW_TILE, D_MODEL), lambda i: (i, 0)),
        out_shape=jax.ShapeDtypeStruct((m, D_MODEL), F32),
        compiler_params=_cparams(1),
        name="out_proj_ln",
    )(*args)


def _conv_kernel(x_ref, w_ref, b_ref, o_ref, xs_scr, *, n_prompt_tiles, len_prompt):
    rows = x_ref.shape[0]
    pad = CONV_PAD
    half = CONV_KERNEL // 2
    zeros = jnp.zeros((pad, xs_scr.shape[1]), F32)
    xs_scr[0:pad, :] = zeros
    xs_scr[pad + rows:, :] = zeros
    xs_scr[pad:pad + rows, :] = x_ref[...].astype(F32)

    def taps(start, n, valid=None):
        acc = b_ref[...]
        for k in range(CONV_KERNEL):
            xk = xs_scr[pl.ds(pad + start + k - half, n), :]
            if valid is not None:
                xk = jnp.where(valid[k], xk, 0.0)
            acc = acc + w_ref[pl.ds(k, 1), :] * xk
        return _silu(acc).astype(o_ref.dtype)

    o_ref[...] = taps(0, rows)

    @pl.when(pl.program_id(0) < n_prompt_tiles)
    def _():
        i = lax.broadcasted_iota(jnp.int32, (2 * pad, 1), 0)
        valid = [(i < pad) == (i + k - half < pad) for k in range(CONV_KERNEL)]
        for b in range(1, rows // len_prompt):
            r = b * len_prompt
            o_ref[r - pad:r + pad, :] = taps(r - pad, 2 * pad, valid)


def _conv_silu(proj, col0, width, conv_w, conv_b, n_prompt_rows, len_prompt, len_latent):
    m = proj.shape[0]
    rows = len_latent
    c0 = col0 // CONV_COLS
    return pl.pallas_call(
        functools.partial(_conv_kernel, n_prompt_tiles=n_prompt_rows // rows, len_prompt=len_prompt),
        grid=(m // rows, width // CONV_COLS),
        in_specs=[
            pl.BlockSpec((rows, CONV_COLS), lambda i, j: (i, c0 + j)),
            pl.BlockSpec((CONV_KERNEL, CONV_COLS), lambda i, j: (0, j)),
            pl.BlockSpec((1, CONV_COLS), lambda i, j: (0, j)),
        ],
        out_specs=pl.BlockSpec((rows, CONV_COLS), lambda i, j: (i, j)),
        out_shape=jax.ShapeDtypeStruct((m, width), BF16),
        scratch_shapes=[pltpu.VMEM((rows + 2 * CONV_PAD, CONV_COLS), F32)],
        compiler_params=_cparams(2),
        name="conv_silu",
    )(proj, conv_w, conv_b.reshape(1, -1))


def _ssd_kernel(*refs, seq_len, has_h0, emit_state, n_unread):
    x_ref, z_ref, b_ref, c_ref, dtt_ref, biast_ref, alogt_ref, dskip_ref = refs[:8]
    pos = 8
    h0_ref = None
    if has_h0:
        h0_ref = refs[pos]
        pos += 1
    pos += n_unread
    y_ref = refs[pos]
    pos += 1
    hfin_ref = None
    if emit_state:
        hfin_ref = refs[pos]
        pos += 1
    s_scr, dtct_scr, acst_scr, dtw_scr, acsw_scr, acsb_scr, y_scr = refs[pos:pos + 7]

    q = A_CHUNK
    nc = seq_len // q
    hpg = A_HEADS_PER_GROUP
    hd = A_HEAD_DIM
    ii = lax.broadcasted_iota(jnp.int32, (q, q), 0)
    jj = lax.broadcasted_iota(jnp.int32, (q, q), 1)
    lower = ii >= jj
    upper = ii <= jj
    t_lower = jnp.where(lower, 1.0, 0.0).astype(BF16)
    t_upper = jnp.where(upper, 1.0, 0.0).astype(BF16)
    t_lower_r = jnp.concatenate([t_lower] * 3, axis=0)
    t_upper_r = jnp.concatenate([t_upper] * 3, axis=0)

    def expander(width):
        er = lax.broadcasted_iota(jnp.int32, (3 * hpg, hpg * width), 0)
        ec = lax.broadcasted_iota(jnp.int32, (3 * hpg, hpg * width), 1)
        return jnp.where(ec // width == er % hpg, 1.0, 0.0).astype(BF16)

    expand_hd = expander(hd)
    expand_q = expander(q)
    first_head = lax.broadcasted_iota(jnp.int32, (q, 2 * hd), 1) < hd

    def pieces(v_t):
        return jnp.concatenate([p.astype(F32) for p in _split3(v_t)], axis=0).astype(BF16)

    y_scr[...] = dskip_ref[...] * x_ref[...].astype(F32)

    for d in range(2):
        bias_t = biast_ref[d * hpg:(d + 1) * hpg, :]
        a_neg_t = -jnp.exp(alogt_ref[d * hpg:(d + 1) * hpg, :])
        sums = t_upper_r if d == 0 else t_lower_r
        dtcs = [_softplus(dtt_ref[c][d * hpg:(d + 1) * hpg, :] + bias_t) for c in range(nc)]
        dta = jnp.concatenate(dtcs, axis=0) * jnp.concatenate([a_neg_t] * nc, axis=0)
        acs = _dot(jnp.concatenate(_split3(dta), axis=1), sums)
        for c in range(nc):
            dtct_scr[d, c] = dtcs[c]
            acst_scr[d, c] = acs[c * hpg:(c + 1) * hpg]
        if has_h0:
            s_scr[d] = jnp.transpose(h0_ref[d].reshape(hpg * hd, A_D_STATE))
        else:
            s_scr[d] = jnp.zeros((A_D_STATE, hpg * hd), F32)

    def spread(step, slot):
        for d in range(2):
            c = step if d == 0 else nc - 1 - step
            acs_p = pieces(acst_scr[d, c])
            dtw_scr[slot, d] = _dot_tn(pieces(dtct_scr[d, c]), expand_hd)
            acsw_scr[slot, d] = _dot_tn(acs_p, expand_hd)
            acsb_scr[slot, d] = _dot_tn(acs_p, expand_q)

    spread(0, 0)

    def step(ci, slot):
        ts = []
        for d in range(2):
            c = ci if d == 0 else nc - 1 - ci
            r0 = pl.multiple_of(c * q, q)
            ts.append(dict(d=d, rows=pl.ds(r0, q), acs_t=acst_scr[d, c], mask=lower if d == 0 else upper,
                           dt_w=dtw_scr[slot, d], acs_w=acsw_scr[slot, d], acs_b=acsb_scr[slot, d]))
        for t in ts:
            t["bb"] = b_ref[t["rows"], :].astype(BF16)
            t["cc"] = c_ref[t["rows"], :].astype(BF16)
            t["cb"] = _dot_nt(t["cc"], t["bb"])
            t["s_old"] = s_scr[t["d"]]
            t["cs"] = _dot(t["cc"], t["s_old"].astype(BF16))
            t["xdt"] = x_ref[t["rows"], :].astype(F32) * t["dt_w"]
            t["xdt_b"] = t["xdt"].astype(BF16)
            t["tot_w"] = t["acs_w"][q - 1:q, :] if t["d"] == 0 else t["acs_w"][0:1, :]
        spread(jnp.minimum(ci + 1, nc - 1), 1 - slot)
        for t in ts:
            t["y_diag"] = []
        for pr in range(hpg // 2):
            for t in ts:
                mms = []
                for r in (2 * pr, 2 * pr + 1):
                    seg = t["acs_b"][:, r * q:(r + 1) * q] - t["acs_t"][r:r + 1, :]
                    lmat = jnp.exp(jnp.where(t["mask"], seg, -jnp.inf))
                    mms.append((t["cb"] * lmat).astype(BF16))
                x_pair = t["xdt_b"][:, 2 * pr * hd:2 * (pr + 1) * hd]
                x_diag = jnp.concatenate([jnp.where(first_head, x_pair, jnp.zeros_like(x_pair)),
                                          jnp.where(first_head, jnp.zeros_like(x_pair), x_pair)], axis=0)
                t["y_diag"].append(_dot(jnp.concatenate(mms, axis=1), x_diag))
        for t in ts:
            y_off = jnp.exp(t["acs_w"]) * t["cs"]
            y_scr[t["rows"], :] += jnp.concatenate(t["y_diag"], axis=1) + y_off
            xs = (t["xdt"] * jnp.exp(t["tot_w"] - t["acs_w"])).astype(BF16)
            s_scr[t["d"]] = t["s_old"] * jnp.exp(t["tot_w"]) + _dot_tn(t["bb"], xs)

    def body(k, carry):
        step(2 * k, 0)
        step(2 * k + 1, 1)
        return carry

    lax.fori_loop(0, nc // 2, body, 0)
    if nc % 2:
        step(nc - 1, 0)
    if emit_state:
        for d in range(2):
            hfin_ref[d] = jnp.transpose(s_scr[d]).reshape(hpg, hd, A_D_STATE)

    y_ref[...] = (y_scr[...] * _silu(z_ref[...].astype(F32))).astype(y_ref.dtype)


def _ssd_core(proj, xbc, dt_row, params, layer, n_layers, h0, prev, states_prev, row0, n_seq, seq_len, emit_state):
    m = proj.shape[0]
    gd = A_GROUP_DIM
    ns = A_D_STATE
    hpg = A_HEADS_PER_GROUP
    sb = row0 // seq_len
    nc = seq_len // A_CHUNK
    x_blocks = A_D_INNER // gd
    bias_r, alog_r, dskip = params
    in_specs = [
        pl.BlockSpec((seq_len, gd), lambda n, g: (sb + n, g)),
        pl.BlockSpec((seq_len, gd), lambda n, g: (sb + n, g)),
        pl.BlockSpec((seq_len, ns), lambda n, g: (sb + n, x_blocks * (gd // ns) + g)),
        pl.BlockSpec((seq_len, ns), lambda n, g: (sb + n, x_blocks * (gd // ns) + A_N_GROUPS + g)),
        pl.BlockSpec((None, nc, 2 * hpg, A_CHUNK), lambda n, g: (g, sb + n, 0, 0)),
        pl.BlockSpec((None, 2 * hpg, 1), lambda n, g: (g, 0, 0)),
        pl.BlockSpec((None, 2 * hpg, 1), lambda n, g: (g, 0, 0)),
        pl.BlockSpec((None, 1, gd), lambda n, g: (g, 0, 0)),
    ]
    args = [xbc, proj, xbc, xbc, dt_row, bias_r, alog_r, dskip]
    aliases = {}
    state_spec = pl.BlockSpec((None, None, 2, hpg, A_HEAD_DIM, ns), lambda n, g: (n, layer, 0, g, 0, 0))
    if h0 is not None:
        in_specs.append(state_spec)
        args.append(h0)
    n_unread = 0
    for arr, out_idx in ((prev, 0), (states_prev, 1)):
        if arr is not None:
            in_specs.append(pl.BlockSpec(memory_space=pl.ANY))
            args.append(arr)
            aliases[len(args) - 1] = out_idx
            n_unread += 1
    out_specs = [pl.BlockSpec((seq_len, gd), lambda n, g: (sb + n, g))]
    out_shape = [jax.ShapeDtypeStruct((m, A_D_INNER), BF16)]
    if emit_state:
        out_specs.append(state_spec)
        out_shape.append(jax.ShapeDtypeStruct((n_seq, n_layers, 2, A_N_HEADS, A_HEAD_DIM, ns), F32))
    res = pl.pallas_call(
        functools.partial(_ssd_kernel, seq_len=seq_len, has_h0=h0 is not None, emit_state=emit_state,
                          n_unread=n_unread),
        grid=(n_seq, A_N_GROUPS),
        in_specs=in_specs,
        out_specs=out_specs,
        out_shape=out_shape,
        scratch_shapes=[pltpu.VMEM((2, ns, gd), F32),
                        pltpu.VMEM((2, nc, hpg, A_CHUNK), F32),
                        pltpu.VMEM((2, nc, hpg, A_CHUNK), F32),
                        pltpu.VMEM((2, 2, A_CHUNK, gd), F32),
                        pltpu.VMEM((2, 2, A_CHUNK, gd), F32),
                        pltpu.VMEM((2, 2, A_CHUNK, hpg * A_CHUNK), F32),
                        pltpu.VMEM((seq_len, gd), F32)],
        input_output_aliases=aliases,
        compiler_params=_cparams(2),
        name="ssd_core",
    )(*args)
    return res if emit_state else (res[0], None)


def _ssd_mixer(y, mod, cond_of_tile, w_in, conv_w, conv_b, dt_bias, a_log, d_skip, layer, state0, states_prev,
               n_prompt, len_prompt, n_latent, len_latent):
    m = y.shape[0]
    mp = n_prompt * len_prompt
    proj = _mod_linear(y, mod, w_in, cond_of_tile)
    xbc = _conv_silu(proj, A_D_INNER, A_CONV_DIM, conv_w, conv_b, mp, len_prompt, len_latent)
    hpg = A_HEADS_PER_GROUP
    dt0 = A_D_INNER + A_CONV_DIM
    dt_raw = proj[:, dt0:dt0 + 2 * A_N_HEADS].astype(F32)
    dt_row = dt_raw.reshape(m // A_CHUNK, A_CHUNK, 2, A_N_GROUPS, hpg).transpose(3, 0, 2, 4, 1)
    dt_row = dt_row.reshape(A_N_GROUPS, m // A_CHUNK, 2 * hpg, A_CHUNK)

    def per_group(p):
        return p.reshape(2, A_N_GROUPS, hpg).transpose(1, 0, 2).reshape(A_N_GROUPS, 2 * hpg)

    bias_g = per_group(dt_bias.astype(F32))
    alog_g = per_group(a_log.astype(F32))
    dskip = jnp.repeat(d_skip.astype(F32), A_HEAD_DIM).reshape(A_N_GROUPS, 1, A_GROUP_DIM)
    params = (bias_g[:, :, None], alog_g[:, :, None], dskip)
    n_layers = state0.shape[1]
    out, st = _ssd_core(proj, xbc, dt_row, params, layer, n_layers, None, None, states_prev, 0, n_prompt,
                        len_prompt, True)
    out, _ = _ssd_core(proj, xbc, dt_row, params, layer, n_layers, state0.astype(F32), out, None, mp,
                       n_latent, len_latent, False)
    return out, st


def _gdn_kernel(*refs, seq_len, has_s0, emit_state):
    q_ref, k_ref, v_ref, z_ref, ab_ref, abt_ref, dtb_ref, alog_ref, ng_ref = refs[:9]
    pos = 9
    s0_ref = None
    if has_s0:
        s0_ref = refs[pos]
        pos += 2
    o_ref = refs[pos]
    pos += 1
    sfin_ref = None
    if emit_state:
        sfin_ref = refs[pos]
        pos += 1
    gq_scr, h_scr, et_scr, s_scr, o_scr = refs[pos:pos + 5]

    hps = GDN_HEADS_PER_STEP
    hk, hv = B_HEAD_K, B_HEAD_V
    head0 = pl.program_id(1) * hps
    ch = B_CHUNK
    nc = seq_len // ch
    ii = lax.broadcasted_iota(jnp.int32, (ch, ch), 0)
    jj = lax.broadcasted_iota(jnp.int32, (ch, ch), 1)
    same_block = (ii // GDN_SOLVE_BLOCK) == (jj // GDN_SOLVE_BLOCK)
    dirs = []
    for d in range(2):
        dirs.append(dict(
            dt_bias=[dtb_ref[d, head0 + hh] for hh in range(hps)],
            a_neg=[-jnp.exp(alog_ref[d, head0 + hh]) for hh in range(hps)],
            incl=(ii >= jj) if d == 0 else (ii <= jj),
            incl_t=(ii <= jj) if d == 0 else (ii >= jj),
            strict=(ii > jj) if d == 0 else (ii < jj)))

    def mm(a, b):
        return _dot(a.astype(BF16), b.astype(BF16))

    def compose(a, b):
        return a + b + mm(a, b)

    n_prep = min(GDN_PREP_UNROLL, nc)

    def prepare(grp, carry):
        chains = []
        for j, hh in [(j, hh) for j in range(n_prep) for hh in range(hps)]:
            c = grp * n_prep + j
            rows = pl.ds(pl.multiple_of(c * ch, ch), ch)
            qc = q_ref[rows, hh * hk:(hh + 1) * hk].astype(F32)
            kc = k_ref[rows, hh * hk:(hh + 1) * hk].astype(F32)
            vc = v_ref[rows, hh * hv:(hh + 1) * hv].astype(F32)
            qn = qc * lax.rsqrt(jnp.sum(qc * qc, axis=-1, keepdims=True) + RMS_EPS) * (B_HEAD_K ** -0.5)
            kn = kc * lax.rsqrt(jnp.sum(kc * kc, axis=-1, keepdims=True) + RMS_EPS)
            knb = kn.astype(BF16)
            qk = _dot_nt(qn.astype(BF16), knb)
            kk = _dot_nt(knb, knb)
            ab = ab_ref[hh, rows, :]
            abt = abt_ref[hh, c]
            for d in range(2):
                p = dirs[d]
                a_neg, dt_bias = p["a_neg"][hh], p["dt_bias"][hh]
                beta = _sigmoid(ab[:, d:d + 1])
                g_col = a_neg * _softplus(ab[:, 2 + d:3 + d] + dt_bias)
                g_row = a_neg * _softplus(abt[2 + d:3 + d, :] + dt_bias)
                gcs = jnp.sum(jnp.where(p["incl"], jnp.broadcast_to(g_row, (ch, ch)), 0.0), axis=1, keepdims=True)
                gcs_t = jnp.sum(jnp.where(p["incl_t"], jnp.broadcast_to(g_col, (ch, ch)), 0.0), axis=0,
                                keepdims=True)
                tot = gcs[ch - 1:ch, :] if d == 0 else gcs[0:1, :]
                decay = jnp.exp(jnp.where(p["incl"], gcs - gcs_t, -jnp.inf))
                a_mat = jnp.where(p["strict"], beta * kk * decay, 0.0)
                a_diag = jnp.where(same_block, a_mat, 0.0)
                e_gcs = jnp.exp(gcs)
                chains.append(dict(
                    d=d, c=c, hh=hh, rows=rows, a_off=a_mat - a_diag, dx=-a_diag, pw=a_diag,
                    wu=jnp.concatenate([kn * (beta * e_gcs), vc * beta], axis=1),
                    kd=kn * jnp.exp(tot - gcs), qd=qn * e_gcs, qkm=(qk * decay).astype(BF16),
                    et=jnp.broadcast_to(jnp.exp(tot), (1, B_HEAD_V))))
        for _ in range(3):
            for t in chains:
                t["pw"] = mm(t["pw"], t["pw"])
            for t in chains:
                t["dx"] = compose(t["dx"], t["pw"])
        for t in chains:
            t["n"] = t["a_off"] + mm(t["dx"], t["a_off"])
        for t in chains:
            t["nn"] = mm(t["n"], t["n"])
        for t in chains:
            t["mx"] = compose(-t["n"], t["nn"])
        for t in chains:
            t["xb"] = compose(t["mx"], t["dx"]).astype(BF16)
        for t in chains:
            t["wu"] = (t["wu"] + _dot(t["xb"], t["wu"].astype(BF16))).astype(BF16)
        for t in chains:
            t["gh"] = _dot(jnp.transpose(t["kd"]).astype(BF16), t["wu"])
            t["qo"] = _dot(t["qkm"], t["wu"])
        for t in chains:
            d, c, hh = t["d"], t["c"], t["hh"]
            gq_scr[d, hh, c, 0:hk, :] = t["gh"][:, :hk].astype(BF16)
            gq_scr[d, hh, c, hk:, :] = (t["qd"] - t["qo"][:, :hk]).astype(BF16)
            h_scr[d, hh, c] = t["gh"][:, hk:]
            et_scr[d, hh, c] = t["et"]
        for fwd, bwd in zip(chains[0::2], chains[1::2]):
            hh = fwd["hh"]
            o_scr[fwd["rows"], hh * hv:(hh + 1) * hv] = fwd["qo"][:, hk:] + bwd["qo"][:, hk:]
        return carry

    lax.fori_loop(0, nc // n_prep, prepare, 0)

    for d in range(2):
        for hh in range(hps):
            if has_s0:
                s_scr[d, hh] = s0_ref[d, hh]
            else:
                s_scr[d, hh] = jnp.zeros((hk, hv), F32)

    def recur(ci, carry):
        olds = {}
        prods = {}
        for d in range(2):
            c = ci if d == 0 else nc - 1 - ci
            for hh in range(hps):
                olds[d, hh] = s_scr[d, hh]
                prods[d, hh] = _dot(gq_scr[d, hh, c], olds[d, hh].astype(BF16))
        for d in range(2):
            c = ci if d == 0 else nc - 1 - ci
            rows = pl.ds(pl.multiple_of(c * ch, ch), ch)
            for hh in range(hps):
                r = prods[d, hh]
                s_scr[d, hh] = olds[d, hh] * et_scr[d, hh, c] - r[:hk] + h_scr[d, hh, c]
                o_scr[rows, hh * hv:(hh + 1) * hv] += r[hk:]
        return carry

    lax.fori_loop(0, nc, recur, 0)
    if emit_state:
        for d in range(2):
            for hh in range(hps):
                sfin_ref[d, hh] = s_scr[d, hh]

    for hh in range(hps):
        cols = slice(hh * hv, (hh + 1) * hv)
        o = o_scr[:, cols]
        o = o * lax.rsqrt(jnp.mean(o * o, axis=-1, keepdims=True) + RMS_EPS) * ng_ref[...]
        o_ref[:, cols] = (o * _silu(z_ref[:, cols].astype(F32))).astype(o_ref.dtype)


def _gdn_core(proj, qkv, ab_col, ab_row, dt_bias, a_log, norm_g, s0, prev, row0, n_seq, seq_len, emit_state):
    m = proj.shape[0]
    hk, hv = B_HEAD_K, B_HEAD_V
    hps = GDN_HEADS_PER_STEP
    wk, wv = hps * hk, hps * hv
    sb = row0 // seq_len
    nc = seq_len // B_CHUNK
    in_specs = [
        pl.BlockSpec((seq_len, wk), lambda n, h: (sb + n, h)),
        pl.BlockSpec((seq_len, wk), lambda n, h: (sb + n, B_QK_DIM // wk + h)),
        pl.BlockSpec((seq_len, wv), lambda n, h: (sb + n, (2 * B_QK_DIM) // wv + h)),
        pl.BlockSpec((seq_len, wv), lambda n, h: (sb + n, B_CONV_DIM // wv + h)),
        pl.BlockSpec((hps, seq_len, 4), lambda n, h: (h, sb + n, 0)),
        pl.BlockSpec((hps, nc, 4, B_CHUNK), lambda n, h: (h, sb + n, 0, 0)),
        pl.BlockSpec(memory_space=pltpu.SMEM),
        pl.BlockSpec(memory_space=pltpu.SMEM),
        pl.BlockSpec((1, hv), lambda n, h: (0, 0)),
    ]
    args = [qkv, qkv, qkv, proj, ab_col, ab_row, dt_bias, a_log, norm_g.reshape(1, hv)]
    aliases = {}
    if s0 is not None:
        in_specs += [pl.BlockSpec((None, 2, hps, hk, hv), lambda n, h: (n, 0, h, 0, 0)),
                     pl.BlockSpec(memory_space=pl.ANY)]
        args += [s0, prev]
        aliases = {len(args) - 1: 0}
    out_specs = [pl.BlockSpec((seq_len, wv), lambda n, h: (sb + n, h))]
    out_shape = [jax.ShapeDtypeStruct((m, B_V_DIM), BF16)]
    if emit_state:
        out_specs.append(pl.BlockSpec((None, 2, hps, hk, hv), lambda n, h: (n, 0, h, 0, 0)))
        out_shape.append(jax.ShapeDtypeStruct((n_seq, 2, B_N_HEADS, hk, hv), F32))
    res = pl.pallas_call(
        functools.partial(_gdn_kernel, seq_len=seq_len, has_s0=s0 is not None, emit_state=emit_state),
        grid=(n_seq, B_N_HEADS // hps),
        in_specs=in_specs,
        out_specs=out_specs,
        out_shape=out_shape,
        scratch_shapes=[
            pltpu.VMEM((2, hps, nc, hk + B_CHUNK, hk), BF16),
            pltpu.VMEM((2, hps, nc, hk, hv), F32),
            pltpu.VMEM((2, hps, nc, 1, hv), F32),
            pltpu.VMEM((2, hps, hk, hv), F32),
            pltpu.VMEM((seq_len, wv), F32),
        ],
        input_output_aliases=aliases,
        compiler_params=_cparams(2),
        name="gdn_core",
    )(*args)
    return res if emit_state else (res[0], None)


def _gdn_mixer(y, mod, cond_of_tile, w_in, conv_w, conv_b, dt_bias, a_log, norm_g, state0, n_prompt, len_prompt,
               n_latent, len_latent):
    m = y.shape[0]
    mp = n_prompt * len_prompt
    proj = _mod_linear(y, mod, w_in, cond_of_tile)
    qkv = _conv_silu(proj, 0, B_CONV_DIM, conv_w, conv_b, mp, len_prompt, len_latent)
    ab0 = B_CONV_DIM + B_V_DIM
    ab = proj[:, ab0:ab0 + 4 * B_N_HEADS].astype(F32).reshape(m, 4, B_N_HEADS)
    ab_col = ab.transpose(2, 0, 1)
    ab_row = ab.reshape(m // B_CHUNK, B_CHUNK, 4, B_N_HEADS).transpose(3, 0, 2, 1)
    dt_bias = dt_bias.astype(F32)
    a_log = a_log.astype(F32)
    out, st = _gdn_core(proj, qkv, ab_col, ab_row, dt_bias, a_log, norm_g, None, None, 0, n_prompt, len_prompt, True)
    out, _ = _gdn_core(proj, qkv, ab_col, ab_row, dt_bias, a_log, norm_g, state0, out, mp, n_latent, len_latent,
                       False)
    return out, st


KV_PAIR = 2
PAIR_Q = KV_PAIR * C_GROUP * C_HEAD_DIM
PAIR_KV = KV_PAIR * C_HEAD_DIM


def _softmax_pv(heads, values):
    mxs = []
    for scores, sink in heads:
        mx = sink
        for s in scores:
            mx = jnp.maximum(mx, jnp.max(s, axis=-1, keepdims=True))
        mxs.append(mx)
    probs = [[jnp.exp(s - mx) for s in scores] for (scores, _), mx in zip(heads, mxs)]
    dens = []
    for (_, sink), mx, ps in zip(heads, mxs, probs):
        den = jnp.exp(sink - mx)
        for p in ps:
            den = den + jnp.sum(p, axis=-1, keepdims=True)
        dens.append(den)
    outs = []
    for ps in probs:
        acc = None
        for p, v in zip(ps, values):
            pv = _dot(p.astype(BF16), v)
            acc = pv if acc is None else acc + pv
        outs.append(acc)
    return [acc / den for acc, den in zip(outs, dens)]


def _attn_ctx_kernel(q_ref, k_ref, v_ref, sink_ref, o_ref):
    pair = pl.program_id(1)
    hd = C_HEAD_DIM
    for gg in range(KV_PAIR):
        kb = k_ref[:, gg * hd:(gg + 1) * hd].astype(BF16)
        vb = v_ref[:, gg * hd:(gg + 1) * hd].astype(BF16)
        heads = []
        for r in range(C_GROUP):
            col = (gg * C_GROUP + r) * hd
            qb = q_ref[:, col:col + hd].astype(BF16)
            sink = sink_ref[pair * (KV_PAIR * C_GROUP) + gg * C_GROUP + r]
            heads.append(([_dot_nt(qb, kb) * C_SCALE], sink))
        for r, o in enumerate(_softmax_pv(heads, [vb])):
            col = (gg * C_GROUP + r) * hd
            o_ref[:, col:col + hd] = o


def _attn_ctx(proj, sink, n_prompt, len_prompt):
    m = proj.shape[0]
    k_block0 = C_Q_DIM // PAIR_KV
    v_block0 = (C_Q_DIM + C_KV_DIM) // PAIR_KV
    return pl.pallas_call(
        _attn_ctx_kernel,
        grid=(n_prompt, C_N_KV // KV_PAIR),
        in_specs=[
            pl.BlockSpec((len_prompt, PAIR_Q), lambda n, p: (n, p)),
            pl.BlockSpec((len_prompt, PAIR_KV), lambda n, p: (n, k_block0 + p)),
            pl.BlockSpec((len_prompt, PAIR_KV), lambda n, p: (n, v_block0 + p)),
            pl.BlockSpec(memory_space=pltpu.SMEM),
        ],
        out_specs=pl.BlockSpec((len_prompt, PAIR_Q), lambda n, p: (n, p)),
        out_shape=jax.ShapeDtypeStruct((m, C_Q_DIM), F32),
        compiler_params=_cparams(2),
        name="attn_ctx",
    )(proj, proj, proj, sink)


def _rope(x, cos, sin):
    width = x.shape[1]
    lane = lax.broadcasted_iota(jnp.int32, x.shape, 1)
    nf = C_HEAD_DIM // 4
    partner = jnp.where(lane % (2 * nf) < nf, pltpu.roll(x, width - nf, 1), pltpu.roll(x, nf, 1))
    return x * cos + partner * sin


def _attn_lat_kernel(q_ref, kp_ref, vp_ref, kc_ref, vc_ref, cq_ref, sq_ref, ck_ref, sk_ref, sink_ref, prev_ref, o_ref,
                     *, seq_len):
    del prev_ref
    pair = pl.program_id(1)
    qi = pl.program_id(2)
    hd = C_HEAD_DIM
    blk = C_BLOCK
    span = C_BLOCK + 2 * C_WINDOW
    start = pl.multiple_of(qi * blk, blk)
    q = _rope(q_ref[...].astype(F32), cq_ref[...], sq_ref[...])
    kw = _rope(kp_ref[pl.ds(start, span), :].astype(F32), ck_ref[pl.ds(start, span), :],
               sk_ref[pl.ds(start, span), :])
    vw = vp_ref[pl.ds(start, span), :]
    iq = lax.broadcasted_iota(jnp.int32, (blk, span), 0)
    jk = lax.broadcasted_iota(jnp.int32, (blk, span), 1)
    kpos = start - C_WINDOW + jk
    ok = jnp.logical_and(jnp.abs(iq + C_WINDOW - jk) <= C_WINDOW, jnp.logical_and(kpos >= 0, kpos < seq_len))
    for gg in range(KV_PAIR):
        kb = kw[:, gg * hd:(gg + 1) * hd].astype(BF16)
        vb = vw[:, gg * hd:(gg + 1) * hd].astype(BF16)
        kcb = kc_ref[:, gg * hd:(gg + 1) * hd].astype(BF16)
        vcb = vc_ref[:, gg * hd:(gg + 1) * hd].astype(BF16)
        heads = []
        for r in range(C_GROUP):
            col = (gg * C_GROUP + r) * hd
            qb = q[:, col:col + hd].astype(BF16)
            s_loc = jnp.where(ok, _dot_nt(qb, kb) * C_SCALE, -jnp.inf)
            s_ctx = _dot_nt(qb, kcb) * C_SCALE
            sink = sink_ref[pair * (KV_PAIR * C_GROUP) + gg * C_GROUP + r]
            heads.append(([s_loc, s_ctx], sink))
        for r, o in enumerate(_softmax_pv(heads, [vb, vcb])):
            col = (gg * C_GROUP + r) * hd
            o_ref[:, col:col + hd] = o


def _rope_tables(seq_len):
    nf = C_HEAD_DIM // 4
    t = jnp.arange(seq_len)
    inv_freq = ROPE_BASE ** (-jnp.arange(nf, dtype=F32) / nf)
    ang_r = (t // GRID_W).astype(F32)[:, None] * inv_freq
    ang_c = (t % GRID_W).astype(F32)[:, None] * inv_freq
    cos = jnp.concatenate([jnp.cos(ang_r), jnp.cos(ang_r), jnp.cos(ang_c), jnp.cos(ang_c)], axis=1)
    sin = jnp.concatenate([-jnp.sin(ang_r), jnp.sin(ang_r), -jnp.sin(ang_c), jnp.sin(ang_c)], axis=1)
    return cos, sin


def _attn_lat(proj, cache_k, cache_v, sink, prev, row0, n_latent, len_latent):
    m = proj.shape[0]
    past = cache_k.shape[1]
    w = C_WINDOW
    k_lat = proj[row0:, C_Q_DIM:C_Q_DIM + C_KV_DIM].reshape(n_latent, len_latent, C_KV_DIM)
    v_lat = proj[row0:, C_Q_DIM + C_KV_DIM:C_Q_DIM + 2 * C_KV_DIM].reshape(n_latent, len_latent, C_KV_DIM)
    kp = jnp.pad(k_lat, ((0, 0), (w, w), (0, 0)))
    vp = jnp.pad(v_lat, ((0, 0), (w, w), (0, 0)))
    kc = cache_k.reshape(n_latent, past, C_KV_DIM).astype(F32)
    vc = cache_v.reshape(n_latent, past, C_KV_DIM).astype(F32)
    cos, sin = _rope_tables(len_latent)
    cos_q = jnp.tile(cos, (1, PAIR_Q // C_HEAD_DIM))
    sin_q = jnp.tile(sin, (1, PAIR_Q // C_HEAD_DIM))
    cos_k = jnp.pad(jnp.tile(cos, (1, KV_PAIR)), ((w, w), (0, 0)))
    sin_k = jnp.pad(jnp.tile(sin, (1, KV_PAIR)), ((w, w), (0, 0)))
    qb0 = row0 // C_BLOCK
    nqb = len_latent // C_BLOCK
    padded = len_latent + 2 * w
    return pl.pallas_call(
        functools.partial(_attn_lat_kernel, seq_len=len_latent),
        grid=(n_latent, C_N_KV // KV_PAIR, nqb),
        in_specs=[
            pl.BlockSpec((C_BLOCK, PAIR_Q), lambda n, p, i: (qb0 + n * nqb + i, p)),
            pl.BlockSpec((None, padded, PAIR_KV), lambda n, p, i: (n, 0, p)),
            pl.BlockSpec((None, padded, PAIR_KV), lambda n, p, i: (n, 0, p)),
            pl.BlockSpec((None, past, PAIR_KV), lambda n, p, i: (n, 0, p)),
            pl.BlockSpec((None, past, PAIR_KV), lambda n, p, i: (n, 0, p)),
            pl.BlockSpec((C_BLOCK, PAIR_Q), lambda n, p, i: (i, 0)),
            pl.BlockSpec((C_BLOCK, PAIR_Q), lambda n, p, i: (i, 0)),
            pl.BlockSpec((padded, PAIR_KV), lambda n, p, i: (0, 0)),
            pl.BlockSpec((padded, PAIR_KV), lambda n, p, i: (0, 0)),
            pl.BlockSpec(memory_space=pltpu.SMEM),
            pl.BlockSpec(memory_space=pl.ANY),
        ],
        out_specs=pl.BlockSpec((C_BLOCK, PAIR_Q), lambda n, p, i: (qb0 + n * nqb + i, p)),
        out_shape=jax.ShapeDtypeStruct((m, C_Q_DIM), F32),
        input_output_aliases={10: 0},
        compiler_params=_cparams(3),
        name="attn_lat",
    )(proj, kp, vp, kc, vc, cos_q, sin_q, cos_k, sin_k, sink, prev)


def _attn_mixer(y, mod, cond_of_tile, w_in, sink, cache_k, cache_v, n_prompt, len_prompt, n_latent, len_latent):
    m = y.shape[0]
    mp = n_prompt * len_prompt
    proj = _mod_linear(y, mod, w_in, cond_of_tile)
    sink = sink.astype(F32)
    out = _attn_ctx(proj, sink, n_prompt, len_prompt)
    out = _attn_lat(proj, cache_k, cache_v, sink, out, mp, n_latent, len_latent)
    new_k = proj[:mp, C_Q_DIM:C_Q_DIM + C_KV_DIM].astype(F32).reshape(n_prompt, len_prompt, C_N_KV, C_HEAD_DIM)
    new_v = proj[:mp, C_Q_DIM + C_KV_DIM:C_Q_DIM + 2 * C_KV_DIM].astype(F32)
    new_v = new_v.reshape(n_prompt, len_prompt, C_N_KV, C_HEAD_DIM)
    return out, new_k, new_v


def kernel(x_prompt, x_sample, state_ssd, state_delta, cache_k, cache_v, c, c_ctx, w_mod, b_mod, ln_g, ln_b, ffn_w_gate, ffn_w_up, ffn_w_down, ssd_w_in, ssd_conv_w, ssd_conv_b, ssd_dt_bias, ssd_a_log, ssd_d, ssd_norm, ssd_w_out, gdn_w_in, gdn_conv_w, gdn_conv_b, gdn_dt_bias, gdn_a_log, gdn_norm, gdn_w_out, attn_w_in, attn_sink, attn_w_out):
    n_prompt, len_prompt, d = x_prompt.shape
    n_latent, len_latent, _ = x_sample.shape
    mp = n_prompt * len_prompt
    assert d == D_MODEL and n_latent + 1 <= N_COND
    assert len_prompt % ROW_TILE == 0 or ROW_TILE % len_prompt == 0
    assert mp % ROW_TILE == 0 and len_latent % ROW_TILE == 0
    assert mp % len_latent == 0 and len_latent % len_prompt == 0 and len_latent % GRID_W == 0

    assert mp % FFN_ROW_TILE == 0 and len_latent % FFN_ROW_TILE == 0

    def cond_of_tile(i, tile_rows=ROW_TILE):
        return _cond_index(i, mp // tile_rows, len_latent // tile_rows)

    cond = jnp.concatenate([c_ctx[None].astype(F32), c.astype(F32),
                            jnp.zeros((N_COND - 1 - n_latent, d), F32)], axis=0)
    mods = _adaln(cond, w_mod, b_mod).reshape(DEPTH, N_COND, N_MOD, d)

    y = (x_prompt.reshape(mp, d).astype(F32), x_sample.reshape(n_latent * len_latent, d).astype(F32))
    w_gate, w_up, w_down = ffn_w_gate.astype(BF16), ffn_w_up.astype(BF16), ffn_w_down.astype(BF16)
    seqs = (n_prompt, len_prompt, n_latent, len_latent)
    ssd_states, gdn_states, k_list, v_list = None, [], [], []
    for i in range(DEPTH):
        mod = mods[i]
        y = _ffn_half(y, mod, 0, ln_g[i, 0], ln_b[i, 0], w_gate, w_up, w_down, i, 0, cond_of_tile, mp)
        kind, j = i % 3, i // 3
        if kind == 0:
            a, ssd_states = _ssd_mixer(y, mod, cond_of_tile, ssd_w_in[j], ssd_conv_w[j], ssd_conv_b[j],
                                       ssd_dt_bias[j], ssd_a_log[j], ssd_d[j], j, state_ssd, ssd_states, *seqs)
            y = _out_proj_ln(a, y, mod, ssd_w_out[j], ln_g[i, 1], ln_b[i, 1], cond_of_tile, rms_g=ssd_norm[j])
        elif kind == 1:
            a, st = _gdn_mixer(y, mod, cond_of_tile, gdn_w_in[j], gdn_conv_w[j], gdn_conv_b[j], gdn_dt_bias[j],
                               gdn_a_log[j], gdn_norm[j], state_delta[:, j], *seqs)
            gdn_states.append(st)
            y = _out_proj_ln(a, y, mod, gdn_w_out[j], ln_g[i, 1], ln_b[i, 1], cond_of_tile)
        else:
            a, kc, vc = _attn_mixer(y, mod, cond_of_tile, attn_w_in[j], attn_sink[j], cache_k[:, j], cache_v[:, j],
                                    *seqs)
            k_list.append(kc)
            v_list.append(vc)
            y = _out_proj_ln(a, y, mod, attn_w_out[j], ln_g[i, 1], ln_b[i, 1], cond_of_tile)
        y = _ffn_half(y, mod, 2, ln_g[i, 2], ln_b[i, 2], w_gate, w_up, w_down, i, 1, cond_of_tile, mp,
                      split_out=i == DEPTH - 1)

    y_prompt = y[0].reshape(n_prompt, len_prompt, d)
    y_sample = y[1].reshape(n_latent, len_latent, d)
    return (y_prompt, y_sample, ssd_states, jnp.stack(gdn_states, axis=1),
            jnp.stack(k_list, axis=1), jnp.stack(v_list, axis=1))
```
